```python
import functools
import jax, jax.numpy as jnp
from jax import lax
import numpy as np

D_MODEL = 1024
BATCH = 8
SEQ = 2048
DEPTH = 2
DEC_BATCH = 32
DEC_SEQ = 32
PAST_LEN = 4096

CHUNK = 64
D_MIX = D_MODEL
N_GROUPS = 4
G = D_MIX // N_GROUPS
HEAD_DIM = 64
N_HEADS_ATT = G // HEAD_DIM
N_HEADS_MEM = 4
MEM_HEAD_DIM = G // N_HEADS_MEM
N_MEM = 256
CONV_A_WIDTH = 3
CONV_B_WIDTH = 31
FFN_CONV_WIDTH = 3
N_PREV_CHUNKS = 8
BAND_CHUNKS = N_PREV_CHUNKS + 1
ATTN_WINDOW = N_PREV_CHUNKS * CHUNK
REL_CLIP = 128
D_FF = ((8 * D_MODEL // 3 + 127) // 128) * 128
D_IN_PROJ = 9 * G
RMS_EPS = 1e-6
LN_EPS = 1e-5
NEG_INF = -1e30

kernel_name = "hymba_style_streaming_conv_band_attn_encoder_step"


def rmsnorm(x, g):
    xf = x.astype(jnp.float32)
    y = xf * lax.rsqrt(jnp.mean(xf * xf, axis=-1, keepdims=True) + RMS_EPS)
    return (y * g.astype(jnp.float32)).astype(x.dtype)


def layernorm(x, g, b):
    xf = x.astype(jnp.float32)
    mu = jnp.mean(xf, axis=-1, keepdims=True)
    xc = xf - mu
    y = xc * lax.rsqrt(jnp.mean(xc * xc, axis=-1, keepdims=True) + LN_EPS)
    return (y * g.astype(jnp.float32) + b.astype(jnp.float32)).astype(x.dtype)


def causal_dwconv(x, prev, w, b=None):
    width, ch = w.shape
    xp = jnp.concatenate([prev.astype(x.dtype), x], axis=1)
    y = lax.conv_general_dilated(xp, w[:, None, :].astype(x.dtype), window_strides=(1,), padding="VALID",
                                 dimension_numbers=("NWC", "WIO", "NWC"), feature_group_count=ch)
    if b is not None:
        y = y + b.astype(x.dtype)
    return y, xp[:, xp.shape[1] - (width - 1):]


def rel_bias_table(rel_bias, dist):
    idx = jnp.clip(dist, -REL_CLIP, REL_CLIP) + REL_CLIP
    return rel_bias[:, idx].astype(jnp.float32)


def band_attention_prompt(q, k, v, rel_bias):
    bsz, t, h, dh = q.shape
    nc = t // CHUNK
    qc = q.reshape(bsz, nc, CHUNK, h, dh)
    pad = ((0, 0), (N_PREV_CHUNKS, 0), (0, 0), (0, 0), (0, 0))
    kc = jnp.pad(k.reshape(bsz, nc, CHUNK, h, dh), pad)
    vc = jnp.pad(v.reshape(bsz, nc, CHUNK, h, dh), pad)
    kb = jnp.concatenate([kc[:, j:j + nc] for j in range(BAND_CHUNKS)], axis=2)
    vb = jnp.concatenate([vc[:, j:j + nc] for j in range(BAND_CHUNKS)], axis=2)
    s = jnp.einsum("bcqhd,bckhd->bchqk", qc, kb).astype(jnp.float32) * (dh ** -0.5)
    qi = jnp.arange(CHUNK)
    kk = jnp.arange(BAND_CHUNKS * CHUNK)
    dist = (N_PREV_CHUNKS * CHUNK + qi[:, None]) - kk[None, :]
    s = s + rel_bias_table(rel_bias, dist)[None, None]
    valid = (jnp.arange(nc)[:, None] + (kk // CHUNK)[None, :]) >= N_PREV_CHUNKS
    s = jnp.where(valid[None, :, None, None, :], s, NEG_INF)
    p = jax.nn.softmax(s, axis=-1).astype(v.dtype)
    o = jnp.einsum("bchqk,bckhd->bcqhd", p, vb)
    return o.reshape(bsz, t, h, dh)


def band_attention_sample(q, k, v, rel_bias, k_cache, v_cache):
    dh = q.shape[-1]
    t = q.shape[1]
    l = k_cache.shape[1]
    kk = jnp.concatenate([k_cache.astype(k.dtype), k], axis=1)
    vv = jnp.concatenate([v_cache.astype(v.dtype), v], axis=1)
    s = jnp.einsum("bqhd,bkhd->bhqk", q, kk).astype(jnp.float32) * (dh ** -0.5)
    dist = (l + jnp.arange(t)[:, None]) - jnp.arange(l + t)[None, :]
    s = s + rel_bias_table(rel_bias, dist)[None]
    p = jax.nn.softmax(s, axis=-1).astype(v.dtype)
    return jnp.einsum("bhqk,bkhd->bqhd", p, vv)


def mem_attention(q, mk, mv):
    s = jnp.einsum("bqhd,bmhd->bhqm", q, mk.astype(q.dtype)).astype(jnp.float32) * (q.shape[-1] ** -0.5)
    p = jax.nn.softmax(s, axis=-1).astype(q.dtype)
    return jnp.einsum("bhqm,bmhd->bqhd", p, mv.astype(q.dtype))


def layer(x, mk, mv, prev_a, prev_b, prev_f, attend,
          norm1_g, w_in, conv_a_w, conv_b_w, conv_b_bias, ln_b_g, ln_b_b, rel_bias,
          grp_norm_g, w_out, norm2_g, w_up, ffn_conv_w, ffn_conv_b, w_down):
    bsz, t, _ = x.shape
    h = rmsnorm(x, norm1_g)
    z = h @ w_in
    a_b, a_c, a_h, b_u, b_g, c_q, c_k, c_v, m_q = jnp.split(z, 9, axis=-1)
    ya, new_a = causal_dwconv(a_c * a_h, prev_a, conv_a_w)
    ya = a_b * ya
    yb, new_b = causal_dwconv(b_u * jax.nn.sigmoid(b_g), prev_b, conv_b_w, conv_b_bias)
    yb = jax.nn.silu(layernorm(yb, ln_b_g, ln_b_b))
    q = c_q.reshape(bsz, t, N_HEADS_ATT, HEAD_DIM)
    k = c_k.reshape(bsz, t, N_HEADS_ATT, HEAD_DIM)
    v = c_v.reshape(bsz, t, N_HEADS_ATT, HEAD_DIM)
    yc = attend(q, k, v, rel_bias).reshape(bsz, t, G)
    ym = mem_attention(m_q.reshape(bsz, t, N_HEADS_MEM, MEM_HEAD_DIM), mk, mv).reshape(bsz, t, G)
    y = jnp.stack([ya, yb, yc, ym], axis=2)
    y = rmsnorm(y, grp_norm_g.reshape(N_GROUPS, G)).reshape(bsz, t, D_MIX)
    x = x + y @ w_out
    h2 = rmsnorm(x, norm2_g)
    u, new_f = causal_dwconv(h2 @ w_up, prev_f, ffn_conv_w, ffn_conv_b)
    val, gate = jnp.split(u, 2, axis=-1)
    x = x + (jax.nn.silu(gate) * val) @ w_down
    return x, new_a, new_b, new_f, k, v


def setup_inputs(seed: int = 0) -> dict:
    key = jax.random.key(seed)
    ks = jax.random.split(key, 32)
    f32 = jnp.float32
    nrm = lambda i, shape, scale: jax.random.normal(ks[i], shape, f32) * scale
    win = min(ATTN_WINDOW, PAST_LEN)
    return {
        "x_prompt": nrm(0, (BATCH, SEQ, D_MODEL), 1.0),
        "x_sample": nrm(1, (DEC_BATCH, DEC_SEQ, D_MODEL), 1.0),
        "cache_conv_a": nrm(2, (DEPTH, DEC_BATCH, CONV_A_WIDTH - 1, G), 1.0),
        "cache_conv_b": nrm(3, (DEPTH, DEC_BATCH, CONV_B_WIDTH - 1, G), 1.0),
        "cache_ffn_conv": nrm(4, (DEPTH, DEC_BATCH, FFN_CONV_WIDTH - 1, 2 * D_FF), 1.0),
        "cache_attn_k": nrm(5, (DEPTH, DEC_BATCH, win, N_HEADS_ATT, HEAD_DIM), 1.0),
        "cache_attn_v": nrm(6, (DEPTH, DEC_BATCH, win, N_HEADS_ATT, HEAD_DIM), 1.0),
        "cache_mem_k": nrm(7, (DEPTH, DEC_BATCH, N_MEM, N_HEADS_MEM, MEM_HEAD_DIM), 1.0),
        "cache_mem_v": nrm(8, (DEPTH, DEC_BATCH, N_MEM, N_HEADS_MEM, MEM_HEAD_DIM), 1.0),
        "mem_prompt": nrm(9, (BATCH, N_MEM, D_MODEL), 1.0),
        "norm1_g": 1.0 + nrm(10, (DEPTH, D_MODEL), 0.05),
        "w_in": nrm(11, (DEPTH, D_MODEL, D_IN_PROJ), D_MODEL ** -0.5),
        "w_mem_kv": nrm(12, (DEPTH, D_MODEL, 2 * G), D_MODEL ** -0.5),
        "conv_a_w": nrm(13, (DEPTH, CONV_A_WIDTH, G), CONV_A_WIDTH ** -0.5),
        "conv_b_w": nrm(14, (DEPTH, CONV_B_WIDTH, G), CONV_B_WIDTH ** -0.5),
        "conv_b_bias": nrm(15, (DEPTH, G), 0.02),
        "ln_b_g": 1.0 + nrm(16, (DEPTH, G), 0.05),
        "ln_b_b": nrm(17, (DEPTH, G), 0.02),
        "rel_bias": nrm(18, (DEPTH, N_HEADS_ATT, 2 * REL_CLIP + 1), 0.1),
        "grp_norm_g": 1.0 + nrm(19, (DEPTH, D_MIX), 0.05),
        "w_out": nrm(20, (DEPTH, D_MIX, D_MODEL), D_MIX ** -0.5),
        "norm2_g": 1.0 + nrm(21, (DEPTH, D_MODEL), 0.05),
        "w_up": nrm(22, (DEPTH, D_MODEL, 2 * D_FF), D_MODEL ** -0.5),
        "ffn_conv_w": nrm(23, (DEPTH, FFN_CONV_WIDTH, 2 * D_FF), FFN_CONV_WIDTH ** -0.5),
        "ffn_conv_b": nrm(24, (DEPTH, 2 * D_FF), 0.02),
        "w_down": nrm(25, (DEPTH, D_FF, D_MODEL), D_FF ** -0.5),
        "final_g": 1.0 + nrm(26, (D_MODEL,), 0.05),
    }


def reference(x_prompt, x_sample, cache_conv_a, cache_conv_b, cache_ffn_conv, cache_attn_k, cache_attn_v,
              cache_mem_k, cache_mem_v, mem_prompt, norm1_g, w_in, w_mem_kv, conv_a_w, conv_b_w, conv_b_bias,
              ln_b_g, ln_b_b, rel_bias, grp_norm_g, w_out, norm2_g, w_up, ffn_conv_w, ffn_conv_b, w_down, final_g):
    bp, tp, _ = x_prompt.shape
    win_p = min(ATTN_WINDOW, tp)
    xp, xs = x_prompt, x_sample
    pa, pb, pf, pk, pv, pmk, pmv = [], [], [], [], [], [], []
    sa, sb, sf, sk, sv = [], [], [], [], []
    for l in range(DEPTH):
        weights = (norm1_g[l], w_in[l], conv_a_w[l], conv_b_w[l], conv_b_bias[l], ln_b_g[l], ln_b_b[l],
                   rel_bias[l], grp_norm_g[l], w_out[l], norm2_g[l], w_up[l], ffn_conv_w[l], ffn_conv_b[l], w_down[l])
        mkv = mem_prompt.astype(xp.dtype) @ w_mem_kv[l]
        mk_p, mv_p = jnp.split(mkv, 2, axis=-1)
        mk_p = mk_p.reshape(bp, N_MEM, N_HEADS_MEM, MEM_HEAD_DIM)
        mv_p = mv_p.reshape(bp, N_MEM, N_HEADS_MEM, MEM_HEAD_DIM)
        zero_a = jnp.zeros((bp, CONV_A_WIDTH - 1, G), xp.dtype)
        zero_b = jnp.zeros((bp, CONV_B_WIDTH - 1, G), xp.dtype)
        zero_f = jnp.zeros((bp, FFN_CONV_WIDTH - 1, 2 * D_FF), xp.dtype)
        xp, na, nb, nf, k_p, v_p = layer(xp, mk_p, mv_p, zero_a, zero_b, zero_f, band_attention_prompt, *weights)
        pa.append(na); pb.append(nb); pf.append(nf)
        pk.append(k_p[:, tp - win_p:]); pv.append(v_p[:, tp - win_p:])
        pmk.append(mk_p); pmv.append(mv_p)
        attend_s = functools.partial(band_attention_sample, k_cache=cache_attn_k[l], v_cache=cache_attn_v[l])
        xs, na, nb, nf, k_s, v_s = layer(xs, cache_mem_k[l], cache_mem_v[l], cache_conv_a[l], cache_conv_b[l],
                                         cache_ffn_conv[l], attend_s, *weights)
        sa.append(na); sb.append(nb); sf.append(nf); sk.append(k_s); sv.append(v_s)
    y_prompt = rmsnorm(xp, final_g)
    y_sample = rmsnorm(xs, final_g)
    return (y_prompt, y_sample,
            jnp.stack(pa), jnp.stack(pb), jnp.stack(pf), jnp.stack(pk), jnp.stack(pv), jnp.stack(pmk), jnp.stack(pmv),
            jnp.stack(sa), jnp.stack(sb), jnp.stack(sf), jnp.stack(sk), jnp.stack(sv))
```

```python
import functools

import numpy as np
import jax
import jax.numpy as jnp
from jax import lax
from jax.experimental import pallas as pl
from jax.experimental.pallas import tpu as pltpu

F32 = jnp.float32
BF16 = jnp.bfloat16

D_MODEL = 1024
G = 256
N_HEADS = 4
HEAD_DIM = 64
CHUNK = 64
N_MEM = 256
N_PREV_CHUNKS = 8
ATTN_WINDOW = N_PREV_CHUNKS * CHUNK
REL_CLIP = 128
CONV_A_WIDTH = 3
CONV_B_WIDTH = 31
FFN_CONV_WIDTH = 3
D_FF = 2816
D_IN_PROJ = 9 * G
RMS_EPS = 1e-6
LN_EPS = 1e-5
NEG_INF = -1e30
QK_SCALE = HEAD_DIM ** -0.5

A_HEAD = 8
B_HEAD = 32
Q_PAIR = 2 * CHUNK
K_PAIR = ATTN_WINDOW + Q_PAIR
FF_CHUNK = 256
N_FF_CHUNKS = D_FF // FF_CHUNK
VMEM_LIMIT = 56 * 1024 * 1024

PROMPT_TILE = 256
SAMPLE_STREAMS = 8


def _rms(x, g):
    return x * lax.rsqrt(jnp.mean(x * x, axis=-1, keepdims=True) + RMS_EPS) * g


def _dot(a, b):
    return jnp.dot(a, b, preferred_element_type=F32)


def _dot_nt(a, b):
    return lax.dot_general(a, b, (((1,), (1,)), ((), ())), preferred_element_type=F32)


def _silu(x):
    return x * jax.nn.sigmoid(x)


def _head_of_lane(rows):
    return lax.broadcasted_iota(jnp.int32, (rows, G), 1) // HEAD_DIM


def _memkv_body(mem_ref, w_ref, mk_ref, mv_ref):
    kv = _dot(mem_ref[0].astype(BF16), w_ref[0])
    mk_ref[0, 0] = kv[:, :G]
    mv_ref[0, 0] = kv[:, G:]


def _memkv(mem_prompt, w_mem_kv_bf):
    depth = w_mem_kv_bf.shape[0]
    b = mem_prompt.shape[0]
    out = jax.ShapeDtypeStruct((depth, b, N_MEM, G), F32)
    return pl.pallas_call(
        _memkv_body,
        grid=(depth, b),
        in_specs=[pl.BlockSpec((1, N_MEM, D_MODEL), lambda l, i: (i, 0, 0)),
                  pl.BlockSpec((1, D_MODEL, 2 * G), lambda l, i: (l, 0, 0))],
        out_specs=[pl.BlockSpec((1, 1, N_MEM, G), lambda l, i: (l, i, 0, 0)),
                   pl.BlockSpec((1, 1, N_MEM, G), lambda l, i: (l, i, 0, 0))],
        out_shape=[out, out],
        compiler_params=pltpu.CompilerParams(dimension_semantics=("arbitrary", "arbitrary")),
        name="memkv",
    )(mem_prompt, w_mem_kv_bf)


def _softmax_parts(s_list):
    m = s_list[0].max(axis=-1, keepdims=True)
    for s in s_list[1:]:
        m = jnp.maximum(m, s.max(axis=-1, keepdims=True))
    e_list = [jnp.exp(s - m) for s in s_list]
    l = e_list[0].sum(axis=-1, keepdims=True)
    for e in e_list[1:]:
        l = l + e.sum(axis=-1, keepdims=True)
    return e_list, 1.0 / l


def _mixer_body(*refs, prompt, s_blk, t_blk):
    if prompt:
        (x_ref, mk_ref, mv_ref, prev_a_ref, prev_b_ref, g1_ref, w_in_ref, caw_ref, cbw_ref, cbb_ref,
         lng_ref, lnb_ref, bias_ref, gng_ref, w_out_ref,
         x1_ref, new_a_ref, new_b_ref, k_out_ref, v_out_ref,
         z_scr, y_scr, stage_a, stage_b, kbuf, vbuf) = refs
    else:
        (x_ref, mk_ref, mv_ref, prev_a_ref, prev_b_ref, ck_ref, cv_ref, g1_ref, w_in_ref, caw_ref, cbw_ref,
         cbb_ref, lng_ref, lnb_ref, bias_ref, bias2_ref, gng_ref, w_out_ref,
         x1_ref, new_a_ref, new_b_ref, k_out_ref, v_out_ref,
         z_scr, y_scr, stage_a, stage_b) = refs
    rows = s_blk * t_blk
    t = pl.program_id(1)

    def group_norm(y, gi):
        g = gng_ref[:, gi * G:(gi + 1) * G]
        return _rms(y, g).astype(BF16)

    x = x_ref[...].reshape(rows, D_MODEL)
    h = _rms(x, g1_ref[...]).astype(BF16)
    z_scr[...] = _dot(h, w_in_ref[...])

    @pl.when(t == 0)
    def _():
        stage_a[:, 0:A_HEAD, :] = prev_a_ref[...]
        stage_b[:, 0:B_HEAD, :] = prev_b_ref[...]

    p = (z_scr[:, G:2 * G] * z_scr[:, 2 * G:3 * G]).reshape(s_blk, t_blk, G)
    stage_a[:, A_HEAD:A_HEAD + t_blk, :] = p
    caw = caw_ref[...]
    conv = (caw[2:3][None] * p
            + caw[1:2][None] * stage_a[:, A_HEAD - 1:A_HEAD - 1 + t_blk, :]
            + caw[0:1][None] * stage_a[:, A_HEAD - 2:A_HEAD - 2 + t_blk, :])
    ya = z_scr[:, 0:G] * conv.reshape(rows, G)
    y_scr[:, 0:G] = group_norm(ya, 0)
    last_a = stage_a[:, t_blk:t_blk + A_HEAD, :]
    new_a_ref[...] = last_a
    stage_a[:, 0:A_HEAD, :] = last_a

    glu = z_scr[:, 3 * G:4 * G] * jax.nn.sigmoid(z_scr[:, 4 * G:5 * G])
    stage_b[:, B_HEAD:B_HEAD + t_blk, :] = glu.reshape(s_blk, t_blk, G)
    cbw = cbw_ref[...]
    first = B_HEAD - (CONV_B_WIDTH - 1)
    s_step = max(1, 64 // t_blk)
    r_step = min(t_blk, 64)
    for s0 in range(0, s_blk, s_step):
        for r0 in range(0, t_blk, r_step):
            acc = None
            for j in range(CONV_B_WIDTH):
                term = cbw[j:j + 1][None] * stage_b[s0:s0 + s_step, first + r0 + j:first + r0 + j + r_step, :]
                acc = term if acc is None else acc + term
            yb = acc.reshape(s_step * r_step, G) + cbb_ref[...]
            mu = jnp.mean(yb, axis=-1, keepdims=True)
            yc = yb - mu
            yb = yc * lax.rsqrt(jnp.mean(yc * yc, axis=-1, keepdims=True) + LN_EPS) * lng_ref[...] + lnb_ref[...]
            row = s0 * t_blk + r0
            y_scr[row:row + s_step * r_step, G:2 * G] = group_norm(_silu(yb), 1)
    last_b = stage_b[:, t_blk:t_blk + B_HEAD, :]
    new_b_ref[...] = last_b
    stage_b[:, 0:B_HEAD, :] = last_b

    k_out_ref[...] = z_scr[:, 6 * G:7 * G].reshape(s_blk, t_blk, G)
    v_out_ref[...] = z_scr[:, 7 * G:8 * G].reshape(s_blk, t_blk, G)

    if prompt:
        @pl.when(t == 0)
        def _():
            kbuf[0:ATTN_WINDOW, :] = jnp.zeros((ATTN_WINDOW, G), BF16)
            vbuf[0:ATTN_WINDOW, :] = jnp.zeros((ATTN_WINDOW, G), BF16)

        row0 = pl.multiple_of(t * t_blk, t_blk)
        kbuf[pl.ds(ATTN_WINDOW + row0, t_blk), :] = z_scr[:, 6 * G:7 * G].astype(BF16)
        vbuf[pl.ds(ATTN_WINDOW + row0, t_blk), :] = z_scr[:, 7 * G:8 * G].astype(BF16)
        mk = mk_ref[0].astype(BF16)
        mv = mv_ref[0].astype(BF16)
        head = _head_of_lane(Q_PAIR)
        for s in range(t_blk // Q_PAIR):
            lo = s * Q_PAIR
            base = pl.multiple_of(row0 + lo, Q_PAIR)
            key_ok = (lax.broadcasted_iota(jnp.int32, (1, K_PAIR), 1) + base) >= ATTN_WINDOW
            kb = kbuf[pl.ds(base, K_PAIR), :]
            vb = vbuf[pl.ds(base, K_PAIR), :]
            qc = z_scr[lo:lo + Q_PAIR, 5 * G:6 * G] * QK_SCALE
            qm = z_scr[lo:lo + Q_PAIR, 8 * G:9 * G] * QK_SCALE
            oc = jnp.zeros((Q_PAIR, G), F32)
            om = jnp.zeros((Q_PAIR, G), F32)
            for hd in range(N_HEADS):
                sel = head == hd
                qh = jnp.where(sel, qc, 0.0).astype(BF16)
                sc = jnp.where(key_ok, _dot_nt(qh, kb) + bias_ref[hd], NEG_INF)
                (e,), inv = _softmax_parts([sc])
                oc = jnp.where(sel, _dot(e.astype(BF16), vb) * inv, oc)
                qh = jnp.where(sel, qm, 0.0).astype(BF16)
                (e,), inv = _softmax_parts([_dot_nt(qh, mk)])
                om = jnp.where(sel, _dot(e.astype(BF16), mv) * inv, om)
            y_scr[lo:lo + Q_PAIR, 2 * G:3 * G] = group_norm(oc, 2)
            y_scr[lo:lo + Q_PAIR, 3 * G:4 * G] = group_norm(om, 3)
    else:
        head = _head_of_lane(t_blk)

        def stack_heads(q):
            return jnp.concatenate([jnp.where(head == hd, q, 0.0) for hd in range(N_HEADS)], axis=0).astype(BF16)

        def unstack_heads(o):
            out = jnp.zeros((t_blk, G), F32)
            for hd in range(N_HEADS):
                out = jnp.where(head == hd, o[hd * t_blk:(hd + 1) * t_blk], out)
            return out

        def stream(i, carry):
            r = pl.multiple_of(i * t_blk, t_blk)
            qs = stack_heads(z_scr[pl.ds(r, t_blk), 5 * G:6 * G] * QK_SCALE)
            kn = z_scr[pl.ds(r, t_blk), 6 * G:7 * G].astype(BF16)
            vn = z_scr[pl.ds(r, t_blk), 7 * G:8 * G].astype(BF16)
            kc = ck_ref[i].astype(BF16)
            vc = cv_ref[i].astype(BF16)
            s_old = _dot_nt(qs, kc) + bias_ref[...]
            s_new = _dot_nt(qs, kn) + bias2_ref[...]
            (e_old, e_new), inv = _softmax_parts([s_old, s_new])
            o = (_dot(e_old.astype(BF16), vc) + _dot(e_new.astype(BF16), vn)) * inv
            y_scr[pl.ds(r, t_blk), 2 * G:3 * G] = group_norm(unstack_heads(o), 2)
            qs = stack_heads(z_scr[pl.ds(r, t_blk), 8 * G:9 * G] * QK_SCALE)
            (e,), inv = _softmax_parts([_dot_nt(qs, mk_ref[i].astype(BF16))])
            o = _dot(e.astype(BF16), mv_ref[i].astype(BF16)) * inv
            y_scr[pl.ds(r, t_blk), 3 * G:4 * G] = group_norm(unstack_heads(o), 3)
            return carry

        lax.fori_loop(0, s_blk, stream, 0)

    x1 = x + _dot(y_scr[...], w_out_ref[...])
    x1_ref[...] = x1.reshape(s_blk, t_blk, D_MODEL)


def _const_spec(shape):
    zeros = (0,) * len(shape)
    return pl.BlockSpec(shape, lambda b, t: zeros)


def _mixer(x, mk, mv, prev_a, prev_b, cache_k, cache_v, weights, bias_tables, *, prompt):
    (g1, w_in, caw, cbw, cbb, lng, lnb, gng, w_out) = weights
    nb, seq, _ = x.shape
    if prompt:
        s_blk, t_blk = 1, PROMPT_TILE
    else:
        s_blk, t_blk = SAMPLE_STREAMS, seq
    grid = (nb // s_blk, seq // t_blk)
    nt = grid[1]
    rows = s_blk * t_blk

    stream_spec = lambda r, c: pl.BlockSpec((s_blk, r, c), lambda b, t: (b, 0, 0))
    tile_spec = lambda c: pl.BlockSpec((s_blk, t_blk, c), lambda b, t: (b, t, 0))
    in_specs = [tile_spec(D_MODEL), stream_spec(N_MEM, G), stream_spec(N_MEM, G),
                stream_spec(A_HEAD, G), stream_spec(B_HEAD, G)]
    args = [x, mk, mv, prev_a, prev_b]
    if not prompt:
        in_specs += [stream_spec(ATTN_WINDOW, G), stream_spec(ATTN_WINDOW, G)]
        args += [cache_k, cache_v]
    consts = [g1, w_in, caw, cbw, cbb, lng, lnb, *bias_tables, gng, w_out]
    in_specs += [_const_spec(c.shape) for c in consts]
    args += consts

    if prompt:
        first_kept = nt - ATTN_WINDOW // t_blk
        kv_spec = pl.BlockSpec((1, t_blk, G), lambda b, t: (b, jnp.maximum(t - first_kept, 0), 0))
        kv_rows = ATTN_WINDOW
    else:
        kv_spec = tile_spec(G)
        kv_rows = seq
    out_specs = [tile_spec(D_MODEL), stream_spec(A_HEAD, G), stream_spec(B_HEAD, G), kv_spec, kv_spec]
    out_shape = [jax.ShapeDtypeStruct((nb, seq, D_MODEL), F32),
                 jax.ShapeDtypeStruct((nb, A_HEAD, G), F32),
                 jax.ShapeDtypeStruct((nb, B_HEAD, G), F32),
                 jax.ShapeDtypeStruct((nb, kv_rows, G), F32),
                 jax.ShapeDtypeStruct((nb, kv_rows, G), F32)]
    scratch = [pltpu.VMEM((rows, D_IN_PROJ), F32),
               pltpu.VMEM((rows, 4 * G), BF16),
               pltpu.VMEM((s_blk, A_HEAD + t_blk, G), F32),
               pltpu.VMEM((s_blk, B_HEAD + t_blk, G), F32)]
    if prompt:
        scratch += [pltpu.VMEM((ATTN_WINDOW + seq, G), BF16), pltpu.VMEM((ATTN_WINDOW + seq, G), BF16)]
    return pl.pallas_call(
        functools.partial(_mixer_body, prompt=prompt, s_blk=s_blk, t_blk=t_blk),
        grid=grid, in_specs=in_specs, out_specs=out_specs, out_shape=out_shape, scratch_shapes=scratch,
        compiler_params=pltpu.CompilerParams(dimension_semantics=("arbitrary", "arbitrary"),
                                             vmem_limit_bytes=VMEM_LIMIT),
        name="mixer_prompt" if prompt else "mixer_sample",
    )(*args)


def _ffn_body(x_ref, prev_ref, g2_ref, wv_ref, wg_ref, cw_ref, cb_ref, wd_ref, gf_ref,
              out_ref, new_f_ref, h_scr, acc_scr, carry, stage, *, s_blk, t_blk, final):
    rows = s_blk * t_blk
    t = pl.program_id(1)
    x = x_ref[...].reshape(rows, D_MODEL)
    h_scr[...] = _rms(x, g2_ref[...]).astype(BF16)
    acc_scr[...] = x

    @pl.when(t == 0)
    def _():
        carry[...] = prev_ref[...]

    def conv(u, col):
        u3 = u.reshape(s_blk, t_blk, FF_CHUNK)
        stage[:, 0:A_HEAD, :] = carry[:, :, pl.ds(col, FF_CHUNK)]
        stage[:, A_HEAD:A_HEAD + t_blk, :] = u3
        cw = cw_ref[:, pl.ds(col, FF_CHUNK)]
        y = (cw[2:3][None] * u3
             + cw[1:2][None] * stage[:, A_HEAD - 1:A_HEAD - 1 + t_blk, :]
             + cw[0:1][None] * stage[:, A_HEAD - 2:A_HEAD - 2 + t_blk, :]
             + cb_ref[:, pl.ds(col, FF_CHUNK)][None])
        carry[:, :, pl.ds(col, FF_CHUNK)] = stage[:, t_blk:t_blk + A_HEAD, :]
        return y.reshape(rows, FF_CHUNK)

    def chunk(j, c):
        col = pl.multiple_of(j * FF_CHUNK, FF_CHUNK)
        hb = h_scr[...]
        val = conv(_dot(hb, wv_ref[j]), col)
        gate = conv(_dot(hb, wg_ref[j]), pl.multiple_of(D_FF + col, FF_CHUNK))
        act = (_silu(gate) * val).astype(BF16)
        acc_scr[...] += _dot(act, wd_ref[j])
        return c

    lax.fori_loop(0, N_FF_CHUNKS, chunk, 0)
    new_f_ref[...] = carry[...]
    out = acc_scr[...]
    if final:
        out = _rms(out, gf_ref[...])
    out_ref[...] = out.reshape(s_blk, t_blk, D_MODEL)


def _ffn(x, prev_f, weights, final_g, *, prompt, final):
    (g2, wv, wg, cw, cb, wd) = weights
    nb, seq, _ = x.shape
    if prompt:
        s_blk, t_blk = 1, PROMPT_TILE
    else:
        s_blk, t_blk = SAMPLE_STREAMS, seq
    grid = (nb // s_blk, seq // t_blk)
    rows = s_blk * t_blk
    tile_spec = pl.BlockSpec((s_blk, t_blk, D_MODEL), lambda b, t: (b, t, 0))
    hist_spec = pl.BlockSpec((s_blk, A_HEAD, 2 * D_FF), lambda b, t: (b, 0, 0))
    consts = [g2, wv, wg, cw, cb, wd, final_g]
    return pl.pallas_call(
        functools.partial(_ffn_body, s_blk=s_blk, t_blk=t_blk, final=final),
        grid=grid,
        in_specs=[tile_spec, hist_spec] + [_const_spec(c.shape) for c in consts],
        out_specs=[tile_spec, hist_spec],
        out_shape=[jax.ShapeDtypeStruct((nb, seq, D_MODEL), F32),
                   jax.ShapeDtypeStruct((nb, A_HEAD, 2 * D_FF), F32)],
        scratch_shapes=[pltpu.VMEM((rows, D_MODEL), BF16),
                        pltpu.VMEM((rows, D_MODEL), F32),
                        pltpu.VMEM((s_blk, A_HEAD, 2 * D_FF), F32),
                        pltpu.VMEM((s_blk, A_HEAD + t_blk, FF_CHUNK), F32)],
        compiler_params=pltpu.CompilerParams(dimension_semantics=("arbitrary", "arbitrary"),
                                             vmem_limit_bytes=VMEM_LIMIT),
        name="ffn_prompt" if prompt else "ffn_sample",
    )(x, prev_f, *consts)


def _rel_index(dist):
    return np.clip(dist, -REL_CLIP, REL_CLIP) + REL_CLIP


def _prompt_bias(rel_bias_l):
    r = np.arange(Q_PAIR)[:, None]
    j = np.arange(K_PAIR)[None, :]
    kk = j - CHUNK * (r // CHUNK)
    in_band = (kk >= 0) & (kk < ATTN_WINDOW + CHUNK)
    idx = _rel_index(ATTN_WINDOW + (r % CHUNK) - kk)
    return jnp.where(in_band[None], rel_bias_l[:, idx], NEG_INF).astype(F32)


def _sample_bias(rel_bias_l, t):
    q = np.arange(t)[:, None]
    k = np.arange(ATTN_WINDOW + t)[None, :]
    tab = rel_bias_l[:, _rel_index(ATTN_WINDOW + q - k)].astype(F32).reshape(N_HEADS * t, ATTN_WINDOW + t)
    return tab[:, :ATTN_WINDOW], tab[:, ATTN_WINDOW:]


def _pad_history(prev, head):
    return jnp.pad(prev, ((0, 0), (head - prev.shape[1], 0), (0, 0)))


def kernel(x_prompt, x_sample, cache_conv_a, cache_conv_b, cache_ffn_conv, cache_attn_k, cache_attn_v, cache_mem_k,
           cache_mem_v, mem_prompt, norm1_g, w_in, w_mem_kv, conv_a_w, conv_b_w, conv_b_bias, ln_b_g, ln_b_b, rel_bias,
           grp_norm_g, w_out, norm2_g, w_up, ffn_conv_w, ffn_conv_b, w_down, final_g):
    depth = w_in.shape[0]
    bp, tp, _ = x_prompt.shape
    bs, ts, _ = x_sample.shape
    assert tp % PROMPT_TILE == 0 and PROMPT_TILE % Q_PAIR == 0 and tp >= ATTN_WINDOW
    assert bs % SAMPLE_STREAMS == 0 and ts % 8 == 0 and ts >= B_HEAD
    assert cache_attn_k.shape[2] == ATTN_WINDOW

    w_in_bf = w_in.astype(BF16)
    w_out_bf = w_out.astype(BF16)
    w_up_val = w_up[:, :, :D_FF].reshape(depth, D_MODEL, N_FF_CHUNKS, FF_CHUNK).transpose(0, 2, 1, 3).astype(BF16)
    w_up_gate = w_up[:, :, D_FF:].reshape(depth, D_MODEL, N_FF_CHUNKS, FF_CHUNK).transpose(0, 2, 1, 3).astype(BF16)
    w_down_bf = w_down.reshape(depth, N_FF_CHUNKS, FF_CHUNK, D_MODEL).astype(BF16)
    mem_k_all, mem_v_all = _memkv(mem_prompt, w_mem_kv.astype(BF16))
    final_row = final_g.reshape(1, D_MODEL)

    zero_a = jnp.zeros((bp, A_HEAD, G), F32)
    zero_b = jnp.zeros((bp, B_HEAD, G), F32)
    zero_f = jnp.zeros((bp, A_HEAD, 2 * D_FF), F32)

    xp, xs = x_prompt, x_sample
    outs = {name: [] for name in ("pa", "pb", "pf", "pk", "pv", "sa", "sb", "sf", "sk", "sv")}
    for l in range(depth):
        mixer_w = (norm1_g[l].reshape(1, D_MODEL), w_in_bf[l], conv_a_w[l], conv_b_w[l],
                   conv_b_bias[l].reshape(1, G), ln_b_g[l].reshape(1, G), ln_b_b[l].reshape(1, G),
                   grp_norm_g[l].reshape(1, 4 * G), w_out_bf[l])
        ffn_w = (norm2_g[l].reshape(1, D_MODEL), w_up_val[l], w_up_gate[l], ffn_conv_w[l],
                 ffn_conv_b[l].reshape(1, 2 * D_FF), w_down_bf[l])
        final = l == depth - 1

        xp, na, nb, k_p, v_p = _mixer(xp, mem_k_all[l], mem_v_all[l], zero_a, zero_b, None, None, mixer_w,
                                      (_prompt_bias(rel_bias[l]),), prompt=True)
        xp, nf = _ffn(xp, zero_f, ffn_w, final_row, prompt=True, final=final)
        outs["pa"].append(na[:, A_HEAD - (CONV_A_WIDTH - 1):])
        outs["pb"].append(nb[:, B_HEAD - (CONV_B_WIDTH - 1):])
        outs["pf"].append(nf[:, A_HEAD - (FFN_CONV_WIDTH - 1):])
        outs["pk"].append(k_p.reshape(bp, ATTN_WINDOW, N_HEADS, HEAD_DIM))
        outs["pv"].append(v_p.reshape(bp, ATTN_WINDOW, N_HEADS, HEAD_DIM))

        xs, na, nb, k_s, v_s = _mixer(xs, cache_mem_k[l].reshape(bs, N_MEM, G), cache_mem_v[l].reshape(bs, N_MEM, G),
                                      _pad_history(cache_conv_a[l], A_HEAD), _pad_history(cache_conv_b[l], B_HEAD),
                                      cache_attn_k[l].reshape(bs, ATTN_WINDOW, G),
                                      cache_attn_v[l].reshape(bs, ATTN_WINDOW, G),
                                      mixer_w, _sample_bias(rel_bias[l], ts), prompt=False)
        xs, nf = _ffn(xs, _pad_history(cache_ffn_conv[l], A_HEAD), ffn_w, final_row, prompt=False, final=final)
        outs["sa"].append(na[:, A_HEAD - (CONV_A_WIDTH - 1):])
        outs["sb"].append(nb[:, B_HEAD - (CONV_B_WIDTH - 1):])
        outs["sf"].append(nf[:, A_HEAD - (FFN_CONV_WIDTH - 1):])
        outs["sk"].append(k_s.reshape(bs, ts, N_HEADS, HEAD_DIM))
        outs["sv"].append(v_s.reshape(bs, ts, N_HEADS, HEAD_DIM))

    st = {name: jnp.stack(v) for name, v in outs.items()}
    mem_shape = (depth, bp, N_MEM, N_HEADS, HEAD_DIM)
    return (xp, xs, st["pa"], st["pb"], st["pf"], st["pk"], st["pv"],
            mem_k_all.reshape(mem_shape), mem_v_all.reshape(mem_shape),
            st["sa"], st["sb"], st["sf"], st["sk"], st["sv"])
```

```python
import functools

import numpy as np
import jax
import jax.numpy as jnp
from jax import lax
from jax.experimental import pallas as pl
from jax.experimental.pallas import tpu as pltpu

F32 = jnp.float32
BF16 = jnp.bfloat16

D_MODEL = 1024
G = 256
N_HEADS = 4
HEAD_DIM = 64
CHUNK = 64
N_MEM = 256
N_PREV_CHUNKS = 8
ATTN_WINDOW = N_PREV_CHUNKS * CHUNK
REL_CLIP = 128
CONV_A_WIDTH = 3
CONV_B_WIDTH = 31
FFN_CONV_WIDTH = 3
D_FF = 2816
D_IN_PROJ = 9 * G
RMS_EPS = 1e-6
LN_EPS = 1e-5
NEG_INF = -1e30
QK_SCALE = HEAD_DIM ** -0.5

SUBLANES = 8
A_HEAD = SUBLANES
B_HEAD = 32
Q_PAIR = 2 * CHUNK
K_PAIR = ATTN_WINDOW + Q_PAIR
FF_CHUNK = 256
N_FF_CHUNKS = D_FF // FF_CHUNK
VMEM_LIMIT = 56 * 1024 * 1024

PROMPT_TILE = 256
SAMPLE_STREAMS = 8


def _rms(x, g):
    return x * lax.rsqrt(jnp.mean(x * x, axis=-1, keepdims=True) + RMS_EPS) * g


def _dot(a, b):
    return jnp.dot(a, b, preferred_element_type=F32)


def _dot_nt(a, b):
    return lax.dot_general(a, b, (((1,), (1,)), ((), ())), preferred_element_type=F32)


def _silu(x):
    return x * jax.nn.sigmoid(x)


def _roll_rows(x, shift):
    shift = shift % x.shape[0]
    return x if shift == 0 else pltpu.roll(x, shift, axis=0)


def _shift_rows(u, hist, s):
    moved = _roll_rows(u, s)
    sub = lax.broadcasted_iota(jnp.int32, (SUBLANES, u.shape[1]), 0)
    top = jnp.where(sub < s, _roll_rows(hist, s), moved[0:SUBLANES])
    return jnp.concatenate([top, moved[SUBLANES:]], axis=0)


def _conv3(u, hist, w):
    return w[2:3] * u + w[1:2] * _shift_rows(u, hist, 1) + w[0:1] * _shift_rows(u, hist, 2)


def _memkv_body(mem_ref, w_ref, mk_ref, mv_ref):
    kv = _dot(mem_ref[0].astype(BF16), w_ref[0])
    mk_ref[0, 0] = kv[:, :G]
    mv_ref[0, 0] = kv[:, G:]


def _memkv(mem_prompt, w_mem_kv_bf):
    depth = w_mem_kv_bf.shape[0]
    b = mem_prompt.shape[0]
    out = jax.ShapeDtypeStruct((depth, b, N_MEM, G), F32)
    return pl.pallas_call(
        _memkv_body,
        grid=(depth, b),
        in_specs=[pl.BlockSpec((1, N_MEM, D_MODEL), lambda l, i: (i, 0, 0)),
                  pl.BlockSpec((1, D_MODEL, 2 * G), lambda l, i: (l, 0, 0))],
        out_specs=[pl.BlockSpec((1, 1, N_MEM, G), lambda l, i: (l, i, 0, 0)),
                   pl.BlockSpec((1, 1, N_MEM, G), lambda l, i: (l, i, 0, 0))],
        out_shape=[out, out],
        compiler_params=pltpu.CompilerParams(dimension_semantics=("arbitrary", "arbitrary")),
        name="memkv",
    )(mem_prompt, w_mem_kv_bf)


def _softmax_parts(s_list):
    m = s_list[0].max(axis=-1, keepdims=True)
    for s in s_list[1:]:
        m = jnp.maximum(m, s.max(axis=-1, keepdims=True))
    e_list = [jnp.exp(s - m) for s in s_list]
    l = e_list[0].sum(axis=-1, keepdims=True)
    for e in e_list[1:]:
        l = l + e.sum(axis=-1, keepdims=True)
    return e_list, 1.0 / l


def _stack_heads(q):
    head = lax.broadcasted_iota(jnp.int32, q.shape, 1) // HEAD_DIM
    return jnp.concatenate([jnp.where(head == hd, q, 0.0) for hd in range(N_HEADS)], axis=0).astype(BF16)


def _unstack_heads(o):
    t = o.shape[0] // N_HEADS
    head = lax.broadcasted_iota(jnp.int32, (t, G), 1) // HEAD_DIM
    out = o[0:t]
    for hd in range(1, N_HEADS):
        out = jnp.where(head == hd, o[hd * t:(hd + 1) * t], out)
    return out


def _conv31_rows(xs, w, t_rows):
    first = B_HEAD - (CONV_B_WIDTH - 1)
    moved = [_roll_rows(xs, -b) for b in range(SUBLANES)]
    out = []
    r_step = min(t_rows, 64)
    for r0 in range(0, t_rows, r_step):
        acc = None
        for j in range(CONV_B_WIDTH):
            off = first + j
            lo = r0 + (off // SUBLANES) * SUBLANES
            term = w[j:j + 1] * moved[off % SUBLANES][lo:lo + r_step]
            acc = term if acc is None else acc + term
        out.append(acc)
    return out[0] if len(out) == 1 else jnp.concatenate(out, axis=0)


def _mixer_body(*refs, prompt, s_blk, t_blk):
    if prompt:
        (x_ref, mk_ref, mv_ref, prev_a_ref, prev_b_ref, g1_ref, w_in_ref, caw_ref, cbw_ref, cbb_ref,
         lng_ref, lnb_ref, bias_ref, gng_ref, w_out_ref,
         x1_ref, new_a_ref, new_b_ref, k_out_ref, v_out_ref,
         y_scr, hist_a, hist_b, kbuf, vbuf) = refs
    else:
        (x_ref, mk_ref, mv_ref, prev_a_ref, prev_b_ref, ck_ref, cv_ref, g1_ref, w_in_ref, caw_ref, cbw_ref,
         cbb_ref, lng_ref, lnb_ref, bias_ref, bias2_ref, gng_ref, w_out_ref,
         x1_ref, new_a_ref, new_b_ref, k_out_ref, v_out_ref,
         y_scr, hist_a, hist_b, zc_scr) = refs
    rows = s_blk * t_blk
    t = pl.program_id(1)

    def group_norm(y, gi):
        return _rms(y, gng_ref[:, gi * G:(gi + 1) * G]).astype(BF16)

    x = x_ref[...].reshape(rows, D_MODEL)
    h = _rms(x, g1_ref[...]).astype(BF16)

    @pl.when(t == 0)
    def _():
        hist_a[...] = prev_a_ref[...]
        hist_b[...] = prev_b_ref[...]

    zc = _dot(h, w_in_ref[:, 5 * G:9 * G])
    k_new = zc[:, G:2 * G]
    v_new = zc[:, 2 * G:3 * G]
    k_out_ref[...] = k_new.reshape(s_blk, t_blk, G)
    v_out_ref[...] = v_new.reshape(s_blk, t_blk, G)

    if prompt:
        @pl.when(t == 0)
        def _():
            kbuf[0:ATTN_WINDOW, :] = jnp.zeros((ATTN_WINDOW, G), BF16)
            vbuf[0:ATTN_WINDOW, :] = jnp.zeros((ATTN_WINDOW, G), BF16)

        row0 = pl.multiple_of(t * t_blk, t_blk)
        kbuf[pl.ds(ATTN_WINDOW + row0, t_blk), :] = k_new.astype(BF16)
        vbuf[pl.ds(ATTN_WINDOW + row0, t_blk), :] = v_new.astype(BF16)
        mk = mk_ref[0].astype(BF16)
        mv = mv_ref[0].astype(BF16)
        for s in range(t_blk // Q_PAIR):
            lo = s * Q_PAIR
            base = pl.multiple_of(row0 + lo, Q_PAIR)
            key_ok = (lax.broadcasted_iota(jnp.int32, (1, K_PAIR), 1) + base) >= ATTN_WINDOW
            qs = _stack_heads(zc[lo:lo + Q_PAIR, 0:G] * QK_SCALE)
            sc = jnp.where(key_ok, _dot_nt(qs, kbuf[pl.ds(base, K_PAIR), :]) + bias_ref[...], NEG_INF)
            (e,), inv = _softmax_parts([sc])
            o = _dot(e.astype(BF16), vbuf[pl.ds(base, K_PAIR), :]) * inv
            y_scr[lo:lo + Q_PAIR, 2 * G:3 * G] = group_norm(_unstack_heads(o), 2)
            qs = _stack_heads(zc[lo:lo + Q_PAIR, 3 * G:4 * G] * QK_SCALE)
            (e,), inv = _softmax_parts([_dot_nt(qs, mk)])
            o = _dot(e.astype(BF16), mv) * inv
            y_scr[lo:lo + Q_PAIR, 3 * G:4 * G] = group_norm(_unstack_heads(o), 3)
    else:
        zc_scr[...] = zc

        def stream(i, carry):
            r = pl.multiple_of(i * t_blk, t_blk)
            qs = _stack_heads(zc_scr[pl.ds(r, t_blk), 0:G] * QK_SCALE)
            kn = zc_scr[pl.ds(r, t_blk), G:2 * G].astype(BF16)
            vn = zc_scr[pl.ds(r, t_blk), 2 * G:3 * G].astype(BF16)
            s_old = _dot_nt(qs, ck_ref[i].astype(BF16)) + bias_ref[...]
            s_new = _dot_nt(qs, kn) + bias2_ref[...]
            (e_old, e_new), inv = _softmax_parts([s_old, s_new])
            o = (_dot(e_old.astype(BF16), cv_ref[i].astype(BF16)) + _dot(e_new.astype(BF16), vn)) * inv
            y_scr[pl.ds(r, t_blk), 2 * G:3 * G] = group_norm(_unstack_heads(o), 2)
            qs = _stack_heads(zc_scr[pl.ds(r, t_blk), 3 * G:4 * G] * QK_SCALE)
            (e,), inv = _softmax_parts([_dot_nt(qs, mk_ref[i].astype(BF16))])
            o = _dot(e.astype(BF16), mv_ref[i].astype(BF16)) * inv
            y_scr[pl.ds(r, t_blk), 3 * G:4 * G] = group_norm(_unstack_heads(o), 3)
            return carry

        lax.fori_loop(0, s_blk, stream, 0)

    za = _dot(h, w_in_ref[:, 0:3 * G])
    caw = caw_ref[...]
    for s in range(s_blk):
        sl = slice(s * t_blk, (s + 1) * t_blk)
        p = za[sl, G:2 * G] * za[sl, 2 * G:3 * G]
        ya = za[sl, 0:G] * _conv3(p, hist_a[s], caw)
        hist_a[s] = p[t_blk - A_HEAD:]
        y_scr[sl, 0:G] = group_norm(ya, 0)
    new_a_ref[...] = hist_a[...]

    zb = _dot(h, w_in_ref[:, 3 * G:5 * G])
    cbw = cbw_ref[...]
    for s in range(s_blk):
        sl = slice(s * t_blk, (s + 1) * t_blk)
        glu = zb[sl, 0:G] * jax.nn.sigmoid(zb[sl, G:2 * G])
        yb = _conv31_rows(jnp.concatenate([hist_b[s], glu], axis=0), cbw, t_blk) + cbb_ref[...]
        hist_b[s] = glu[t_blk - B_HEAD:]
        mu = jnp.mean(yb, axis=-1, keepdims=True)
        yc = yb - mu
        yb = yc * lax.rsqrt(jnp.mean(yc * yc, axis=-1, keepdims=True) + LN_EPS) * lng_ref[...] + lnb_ref[...]
        y_scr[sl, G:2 * G] = group_norm(_silu(yb), 1)
    new_b_ref[...] = hist_b[...]

    x1 = x + _dot(y_scr[...], w_out_ref[...])
    x1_ref[...] = x1.reshape(s_blk, t_blk, D_MODEL)


def _const_spec(shape):
    zeros = (0,) * len(shape)
    return pl.BlockSpec(shape, lambda b, t: zeros)


def _mixer(x, mk, mv, prev_a, prev_b, cache_k, cache_v, weights, bias_tables, *, prompt):
    (g1, w_in, caw, cbw, cbb, lng, lnb, gng, w_out) = weights
    nb, seq, _ = x.shape
    if prompt:
        s_blk, t_blk = 1, PROMPT_TILE
    else:
        s_blk, t_blk = SAMPLE_STREAMS, seq
    grid = (nb // s_blk, seq // t_blk)
    nt = grid[1]
    rows = s_blk * t_blk

    stream_spec = lambda r, c: pl.BlockSpec((s_blk, r, c), lambda b, t: (b, 0, 0))
    tile_spec = lambda c: pl.BlockSpec((s_blk, t_blk, c), lambda b, t: (b, t, 0))
    in_specs = [tile_spec(D_MODEL), stream_spec(N_MEM, G), stream_spec(N_MEM, G),
                stream_spec(A_HEAD, G), stream_spec(B_HEAD, G)]
    args = [x, mk, mv, prev_a, prev_b]
    if not prompt:
        in_specs += [stream_spec(ATTN_WINDOW, G), stream_spec(ATTN_WINDOW, G)]
        args += [cache_k, cache_v]
    consts = [g1, w_in, caw, cbw, cbb, lng, lnb, *bias_tables, gng, w_out]
    in_specs += [_const_spec(c.shape) for c in consts]
    args += consts

    if prompt:
        first_kept = nt - ATTN_WINDOW // t_blk
        kv_spec = pl.BlockSpec((1, t_blk, G), lambda b, t: (b, jnp.maximum(t - first_kept, 0), 0))
        kv_rows = ATTN_WINDOW
    else:
        kv_spec = tile_spec(G)
        kv_rows = seq
    out_specs = [tile_spec(D_MODEL), stream_spec(A_HEAD, G), stream_spec(B_HEAD, G), kv_spec, kv_spec]
    out_shape = [jax.ShapeDtypeStruct((nb, seq, D_MODEL), F32),
                 jax.ShapeDtypeStruct((nb, A_HEAD, G), F32),
                 jax.ShapeDtypeStruct((nb, B_HEAD, G), F32),
                 jax.ShapeDtypeStruct((nb, kv_rows, G), F32),
                 jax.ShapeDtypeStruct((nb, kv_rows, G), F32)]
    scratch = [pltpu.VMEM((rows, 4 * G), BF16),
               pltpu.VMEM((s_blk, A_HEAD, G), F32),
               pltpu.VMEM((s_blk, B_HEAD, G), F32)]
    if prompt:
        scratch += [pltpu.VMEM((ATTN_WINDOW + seq, G), BF16), pltpu.VMEM((ATTN_WINDOW + seq, G), BF16)]
    else:
        scratch += [pltpu.VMEM((rows, 4 * G), F32)]
    return pl.pallas_call(
        functools.partial(_mixer_body, prompt=prompt, s_blk=s_blk, t_blk=t_blk),
        grid=grid, in_specs=in_specs, out_specs=out_specs, out_shape=out_shape, scratch_shapes=scratch,
        compiler_params=pltpu.CompilerParams(dimension_semantics=("arbitrary", "arbitrary"),
                                             vmem_limit_bytes=VMEM_LIMIT),
        name="mixer_prompt" if prompt else "mixer_sample",
    )(*args)


def _ffn_body(x_ref, prev_ref, g2_ref, wu_ref, cw_ref, cb_ref, wd_ref, gf_ref,
              out_ref, new_f_ref, act_scr, carry, *, s_blk, t_blk, final):
    rows = s_blk * t_blk
    t = pl.program_id(1)
    x = x_ref[...].reshape(rows, D_MODEL)
    h = _rms(x, g2_ref[...]).astype(BF16)

    @pl.when(t == 0)
    def _():
        carry[...] = prev_ref[...]

    for j in range(N_FF_CHUNKS):
        u = _dot(h, wu_ref[j])
        halves = []
        for half, col in ((0, j * FF_CHUNK), (1, D_FF + j * FF_CHUNK)):
            cw = cw_ref[:, col:col + FF_CHUNK]
            cb = cb_ref[:, col:col + FF_CHUNK]
            per_stream = []
            for s in range(s_blk):
                us = u[s * t_blk:(s + 1) * t_blk, half * FF_CHUNK:(half + 1) * FF_CHUNK]
                per_stream.append(_conv3(us, carry[s, :, col:col + FF_CHUNK], cw) + cb)
                carry[s, :, col:col + FF_CHUNK] = us[t_blk - A_HEAD:]
            halves.append(per_stream[0] if s_blk == 1 else jnp.concatenate(per_stream, axis=0))
        val, gate = halves
        act_scr[:, j * FF_CHUNK:(j + 1) * FF_CHUNK] = (_silu(gate) * val).astype(BF16)

    new_f_ref[...] = carry[...]
    out = x + _dot(act_scr[...], wd_ref[...])
    if final:
        out = _rms(out, gf_ref[...])
    out_ref[...] = out.reshape(s_blk, t_blk, D_MODEL)


def _ffn(x, prev_f, weights, final_g, *, prompt, final):
    (g2, wu, cw, cb, wd) = weights
    nb, seq, _ = x.shape
    if prompt:
        s_blk, t_blk = 1, PROMPT_TILE
    else:
        s_blk, t_blk = SAMPLE_STREAMS, seq
    grid = (nb // s_blk, seq // t_blk)
    rows = s_blk * t_blk
    tile_spec = pl.BlockSpec((s_blk, t_blk, D_MODEL), lambda b, t: (b, t, 0))
    hist_spec = pl.BlockSpec((s_blk, A_HEAD, 2 * D_FF), lambda b, t: (b, 0, 0))
    consts = [g2, wu, cw, cb, wd, final_g]
    return pl.pallas_call(
        functools.partial(_ffn_body, s_blk=s_blk, t_blk=t_blk, final=final),
        grid=grid,
        in_specs=[tile_spec, hist_spec] + [_const_spec(c.shape) for c in consts],
        out_specs=[tile_spec, hist_spec],
        out_shape=[jax.ShapeDtypeStruct((nb, seq, D_MODEL), F32),
                   jax.ShapeDtypeStruct((nb, A_HEAD, 2 * D_FF), F32)],
        scratch_shapes=[pltpu.VMEM((rows, D_FF), BF16),
                        pltpu.VMEM((s_blk, A_HEAD, 2 * D_FF), F32)],
        compiler_params=pltpu.CompilerParams(dimension_semantics=("arbitrary", "arbitrary"),
                                             vmem_limit_bytes=VMEM_LIMIT),
        name="ffn_prompt" if prompt else "ffn_sample",
    )(x, prev_f, *consts)


def _rel_bias_toeplitz(rel_bias_l, n_q, n_k):
    lo = ATTN_WINDOW - (n_k - 1) + REL_CLIP
    hi = ATTN_WINDOW + (n_q - 1) + REL_CLIP
    pad_lo, pad_hi = max(0, -lo), max(0, hi - 2 * REL_CLIP)
    ext = jnp.pad(rel_bias_l, ((0, 0), (pad_lo, pad_hi)), mode="edge")[:, lo + pad_lo:hi + pad_lo + 1]
    period = n_q + n_k
    w = jnp.pad(ext[:, ::-1], ((0, 0), (0, 1)))
    flat = jnp.tile(w, (1, n_q))[:, :n_q * (period - 1)]
    return flat.reshape(-1, n_q, period - 1)[:, :, n_q - 1:n_q - 1 + n_k].astype(F32)


def _prompt_bias(rel_bias_l):
    r = np.arange(Q_PAIR)[:, None]
    kk = np.arange(K_PAIR)[None, :] - CHUNK * (r // CHUNK)
    in_band = (kk >= 0) & (kk < ATTN_WINDOW + CHUNK)
    tab = jnp.where(in_band[None], _rel_bias_toeplitz(rel_bias_l, Q_PAIR, K_PAIR), NEG_INF)
    return tab.reshape(N_HEADS * Q_PAIR, K_PAIR)


def _sample_bias(rel_bias_l, t):
    tab = _rel_bias_toeplitz(rel_bias_l, t, ATTN_WINDOW + t).reshape(N_HEADS * t, ATTN_WINDOW + t)
    return tab[:, :ATTN_WINDOW], tab[:, ATTN_WINDOW:]


def _pad_history(prev, head):
    return jnp.pad(prev, ((0, 0), (head - prev.shape[1], 0), (0, 0)))


def kernel(x_prompt, x_sample, cache_conv_a, cache_conv_b, cache_ffn_conv, cache_attn_k, cache_attn_v, cache_mem_k,
           cache_mem_v, mem_prompt, norm1_g, w_in, w_mem_kv, conv_a_w, conv_b_w, conv_b_bias, ln_b_g, ln_b_b, rel_bias,
           grp_norm_g, w_out, norm2_g, w_up, ffn_conv_w, ffn_conv_b, w_down, final_g):
    depth = w_in.shape[0]
    bp, tp, _ = x_prompt.shape
    bs, ts, _ = x_sample.shape
    assert tp % PROMPT_TILE == 0 and PROMPT_TILE % Q_PAIR == 0 and tp >= ATTN_WINDOW
    assert bs % SAMPLE_STREAMS == 0 and ts % SUBLANES == 0 and ts >= B_HEAD
    assert cache_attn_k.shape[2] == ATTN_WINDOW

    w_in_bf = w_in.astype(BF16)
    w_out_bf = w_out.astype(BF16)
    w_up_bf = (w_up.reshape(depth, D_MODEL, 2, N_FF_CHUNKS, FF_CHUNK).transpose(0, 3, 1, 2, 4)
               .reshape(depth, N_FF_CHUNKS, D_MODEL, 2 * FF_CHUNK).astype(BF16))
    w_down_bf = w_down.astype(BF16)
    mem_k_all, mem_v_all = _memkv(mem_prompt, w_mem_kv.astype(BF16))
    final_row = final_g.reshape(1, D_MODEL)

    zero_a = jnp.zeros((bp, A_HEAD, G), F32)
    zero_b = jnp.zeros((bp, B_HEAD, G), F32)
    zero_f = jnp.zeros((bp, A_HEAD, 2 * D_FF), F32)

    xp, xs = x_prompt, x_sample
    outs = {name: [] for name in ("pa", "pb", "pf", "pk", "pv", "sa", "sb", "sf", "sk", "sv")}
    for l in range(depth):
        mixer_w = (norm1_g[l].reshape(1, D_MODEL), w_in_bf[l], conv_a_w[l], conv_b_w[l],
                   conv_b_bias[l].reshape(1, G), ln_b_g[l].reshape(1, G), ln_b_b[l].reshape(1, G),
                   grp_norm_g[l].reshape(1, 4 * G), w_out_bf[l])
        ffn_w = (norm2_g[l].reshape(1, D_MODEL), w_up_bf[l], ffn_conv_w[l],
                 ffn_conv_b[l].reshape(1, 2 * D_FF), w_down_bf[l])
        final = l == depth - 1

        xp, na, nb, k_p, v_p = _mixer(xp, mem_k_all[l], mem_v_all[l], zero_a, zero_b, None, None, mixer_w,
                                      (_prompt_bias(rel_bias[l]),), prompt=True)
        xp, nf = _ffn(xp, zero_f, ffn_w, final_row, prompt=True, final=final)
        outs["pa"].append(na[:, A_HEAD - (CONV_A_WIDTH - 1):])
        outs["pb"].append(nb[:, B_HEAD - (CONV_B_WIDTH - 1):])
        outs["pf"].append(nf[:, A_HEAD - (FFN_CONV_WIDTH - 1):])
        outs["pk"].append(k_p.reshape(bp, ATTN_WINDOW, N_HEADS, HEAD_DIM))
        outs["pv"].append(v_p.reshape(bp, ATTN_WINDOW, N_HEADS, HEAD_DIM))

        xs, na, nb, k_s, v_s = _mixer(xs, cache_mem_k[l].reshape(bs, N_MEM, G), cache_mem_v[l].reshape(bs, N_MEM, G),
                                      _pad_history(cache_conv_a[l], A_HEAD), _pad_history(cache_conv_b[l], B_HEAD),
                                      cache_attn_k[l].reshape(bs, ATTN_WINDOW, G),
                                      cache_attn_v[l].reshape(bs, ATTN_WINDOW, G),
                                      mixer_w, _sample_bias(rel_bias[l], ts), prompt=False)
        xs, nf = _ffn(xs, _pad_history(cache_ffn_conv[l], A_HEAD), ffn_w, final_row, prompt=False, final=final)
        outs["sa"].append(na[:, A_HEAD - (CONV_A_WIDTH - 1):])
        outs["sb"].append(nb[:, B_HEAD - (CONV_B_WIDTH - 1):])
        outs["sf"].append(nf[:, A_HEAD - (FFN_CONV_WIDTH - 1):])
        outs["sk"].append(k_s.reshape(bs, ts, N_HEADS, HEAD_DIM))
        outs["sv"].append(v_s.reshape(bs, ts, N_HEADS, HEAD_DIM))

    st = {name: jnp.stack(v) for name, v in outs.items()}
    mem_shape = (depth, bp, N_MEM, N_HEADS, HEAD_DIM)
    return (xp, xs, st["pa"], st["pb"], st["pf"], st["pk"], st["pv"],
            mem_k_all.reshape(mem_shape), mem_v_all.reshape(mem_shape),
            st["sa"], st["sb"], st["sf"], st["sk"], st["sv"])
```

```python
import functools

import numpy as np
import jax
import jax.numpy as jnp
from jax import lax
from jax.experimental import pallas as pl
from jax.experimental.pallas import tpu as pltpu

F32 = jnp.float32
BF16 = jnp.bfloat16

D_MODEL = 1024
G = 256
N_HEADS = 4
HEAD_DIM = 64
CHUNK = 64
N_MEM = 256
N_PREV_CHUNKS = 8
ATTN_WINDOW = N_PREV_CHUNKS * CHUNK
REL_CLIP = 128
CONV_A_WIDTH = 3
CONV_B_WIDTH = 31
FFN_CONV_WIDTH = 3
D_FF = 2816
D_IN_PROJ = 9 * G
RMS_EPS = 1e-6
LN_EPS = 1e-5
NEG_INF = -1e30
QK_SCALE = HEAD_DIM ** -0.5

SUBLANES = 8
A_HEAD = SUBLANES
B_HEAD = 32
Q_PAIR = 2 * CHUNK
K_PAIR = ATTN_WINDOW + Q_PAIR
FF_CHUNK = 256
N_FF_CHUNKS = D_FF // FF_CHUNK
VMEM_LIMIT = 56 * 1024 * 1024

PROMPT_TILE = 512
SUB_TILE = 256
SAMPLE_STREAMS = 8


def _rms(x, g):
    return x * lax.rsqrt(jnp.mean(x * x, axis=-1, keepdims=True) + RMS_EPS) * g


def _dot(a, b):
    return jnp.dot(a, b, preferred_element_type=F32)


def _dot_nt(a, b):
    return lax.dot_general(a, b, (((1,), (1,)), ((), ())), preferred_element_type=F32)


def _silu(x):
    return x * jax.nn.sigmoid(x)


def _roll_rows(x, shift):
    shift = shift % x.shape[0]
    return x if shift == 0 else pltpu.roll(x, shift, axis=0)


def _shift_rows(u, hist, s):
    moved = _roll_rows(u, s)
    sub = lax.broadcasted_iota(jnp.int32, (SUBLANES, u.shape[1]), 0)
    top = jnp.where(sub < s, _roll_rows(hist, s), moved[0:SUBLANES])
    return jnp.concatenate([top, moved[SUBLANES:]], axis=0)


def _conv3(u, hist, w):
    return w[2:3] * u + w[1:2] * _shift_rows(u, hist, 1) + w[0:1] * _shift_rows(u, hist, 2)


def _memkv_body(mem_ref, w_ref, mk_ref, mv_ref):
    kv = _dot(mem_ref[0].astype(BF16), w_ref[0])
    mk_ref[0, 0] = kv[:, :G]
    mv_ref[0, 0] = kv[:, G:]


def _memkv(mem_prompt, w_mem_kv_bf):
    depth = w_mem_kv_bf.shape[0]
    b = mem_prompt.shape[0]
    out = jax.ShapeDtypeStruct((depth, b, N_MEM, G), F32)
    return pl.pallas_call(
        _memkv_body,
        grid=(depth, b),
        in_specs=[pl.BlockSpec((1, N_MEM, D_MODEL), lambda l, i: (i, 0, 0)),
                  pl.BlockSpec((1, D_MODEL, 2 * G), lambda l, i: (l, 0, 0))],
        out_specs=[pl.BlockSpec((1, 1, N_MEM, G), lambda l, i: (l, i, 0, 0)),
                   pl.BlockSpec((1, 1, N_MEM, G), lambda l, i: (l, i, 0, 0))],
        out_shape=[out, out],
        compiler_params=pltpu.CompilerParams(dimension_semantics=("arbitrary", "arbitrary")),
        name="memkv",
    )(mem_prompt, w_mem_kv_bf)


def _softmax_parts(s_list):
    m = s_list[0].max(axis=-1, keepdims=True)
    for s in s_list[1:]:
        m = jnp.maximum(m, s.max(axis=-1, keepdims=True))
    e_list = [jnp.exp(s - m) for s in s_list]
    l = e_list[0].sum(axis=-1, keepdims=True)
    for e in e_list[1:]:
        l = l + e.sum(axis=-1, keepdims=True)
    return e_list, 1.0 / l


def _stack_heads(q):
    head = lax.broadcasted_iota(jnp.int32, q.shape, 1) // HEAD_DIM
    return jnp.concatenate([jnp.where(head == hd, q, 0.0) for hd in range(N_HEADS)], axis=0).astype(BF16)


def _unstack_heads(o):
    t = o.shape[0] // N_HEADS
    head = lax.broadcasted_iota(jnp.int32, (t, G), 1) // HEAD_DIM
    out = o[0:t]
    for hd in range(1, N_HEADS):
        out = jnp.where(head == hd, o[hd * t:(hd + 1) * t], out)
    return out


def _conv31_rows(xs, w, t_rows):
    first = B_HEAD - (CONV_B_WIDTH - 1)
    moved = [_roll_rows(xs, -b) for b in range(SUBLANES)]
    out = []
    r_step = min(t_rows, 64)
    for r0 in range(0, t_rows, r_step):
        acc = None
        for j in range(CONV_B_WIDTH):
            off = first + j
            lo = r0 + (off // SUBLANES) * SUBLANES
            term = w[j:j + 1] * moved[off % SUBLANES][lo:lo + r_step]
            acc = term if acc is None else acc + term
        out.append(acc)
    return out[0] if len(out) == 1 else jnp.concatenate(out, axis=0)


def _mixer_body(*refs, prompt, s_blk, t_blk):
    if prompt:
        (x_ref, mk_ref, mv_ref, prev_a_ref, prev_b_ref, g1_ref, w_in_ref, caw_ref, cbw_ref, cbb_ref,
         lng_ref, lnb_ref, bias_ref, gng_ref, w_out_ref,
         x1_ref, new_a_ref, new_b_ref, k_out_ref, v_out_ref,
         hist_a, hist_b, kbuf, vbuf) = refs
    else:
        (x_ref, mk_ref, mv_ref, prev_a_ref, prev_b_ref, ck_ref, cv_ref, g1_ref, w_in_ref, caw_ref, cbw_ref,
         cbb_ref, lng_ref, lnb_ref, bias_ref, bias2_ref, gng_ref, w_out_ref,
         x1_ref, new_a_ref, new_b_ref, k_out_ref, v_out_ref,
         hist_a, hist_b, zc_scr, y_scr) = refs
    sub = min(t_blk, SUB_TILE)
    rows = s_blk * sub
    t = pl.program_id(1)

    def group_norm(y, gi):
        return _rms(y, gng_ref[:, gi * G:(gi + 1) * G]).astype(BF16)

    @pl.when(t == 0)
    def _():
        hist_a[...] = prev_a_ref[...]
        hist_b[...] = prev_b_ref[...]

    def groups_ab(h, ha, hb):
        za = _dot(h, w_in_ref[:, 0:3 * G])
        caw = caw_ref[...]
        ya, ha_new = [], []
        for s in range(s_blk):
            sl = slice(s * sub, (s + 1) * sub)
            p = za[sl, G:2 * G] * za[sl, 2 * G:3 * G]
            ya.append(group_norm(za[sl, 0:G] * _conv3(p, ha[s], caw), 0))
            ha_new.append(p[sub - A_HEAD:])
        zb = _dot(h, w_in_ref[:, 3 * G:5 * G])
        cbw = cbw_ref[...]
        yb, hb_new = [], []
        for s in range(s_blk):
            sl = slice(s * sub, (s + 1) * sub)
            glu = zb[sl, 0:G] * jax.nn.sigmoid(zb[sl, G:2 * G])
            c = _conv31_rows(jnp.concatenate([hb[s], glu], axis=0), cbw, sub) + cbb_ref[...]
            hb_new.append(glu[sub - B_HEAD:])
            cc = c - jnp.mean(c, axis=-1, keepdims=True)
            c = cc * lax.rsqrt(jnp.mean(cc * cc, axis=-1, keepdims=True) + LN_EPS) * lng_ref[...] + lnb_ref[...]
            yb.append(group_norm(_silu(c), 1))
        return jnp.concatenate(ya, axis=0), jnp.concatenate(yb, axis=0), ha_new, hb_new

    def groups_cm(h, r0):
        zc = _dot(h, w_in_ref[:, 5 * G:9 * G])
        k_new = zc[:, G:2 * G]
        v_new = zc[:, 2 * G:3 * G]
        k_out_ref[:, r0:r0 + sub, :] = k_new.reshape(s_blk, sub, G)
        v_out_ref[:, r0:r0 + sub, :] = v_new.reshape(s_blk, sub, G)
        if not prompt:
            zc_scr[...] = zc

            def stream(i, carry):
                r = pl.multiple_of(i * t_blk, t_blk)
                qs = _stack_heads(zc_scr[pl.ds(r, t_blk), 0:G] * QK_SCALE)
                kn = zc_scr[pl.ds(r, t_blk), G:2 * G].astype(BF16)
                vn = zc_scr[pl.ds(r, t_blk), 2 * G:3 * G].astype(BF16)
                s_old = _dot_nt(qs, ck_ref[i].astype(BF16)) + bias_ref[...]
                s_new = _dot_nt(qs, kn) + bias2_ref[...]
                (e_old, e_new), inv = _softmax_parts([s_old, s_new])
                o = (_dot(e_old.astype(BF16), cv_ref[i].astype(BF16)) + _dot(e_new.astype(BF16), vn)) * inv
                y_scr[pl.ds(r, t_blk), 0:G] = group_norm(_unstack_heads(o), 2)
                qs = _stack_heads(zc_scr[pl.ds(r, t_blk), 3 * G:4 * G] * QK_SCALE)
                (e,), inv = _softmax_parts([_dot_nt(qs, mk_ref[i].astype(BF16))])
                o = _dot(e.astype(BF16), mv_ref[i].astype(BF16)) * inv
                y_scr[pl.ds(r, t_blk), G:2 * G] = group_norm(_unstack_heads(o), 3)
                return carry

            lax.fori_loop(0, s_blk, stream, 0)
            return y_scr[:, 0:G], y_scr[:, G:2 * G]

        @pl.when(t == 0)
        def _():
            kbuf[0:ATTN_WINDOW, :] = jnp.zeros((ATTN_WINDOW, G), BF16)
            vbuf[0:ATTN_WINDOW, :] = jnp.zeros((ATTN_WINDOW, G), BF16)

        row0 = pl.multiple_of(t * t_blk + r0, sub)
        kbuf[pl.ds(ATTN_WINDOW + row0, sub), :] = k_new.astype(BF16)
        vbuf[pl.ds(ATTN_WINDOW + row0, sub), :] = v_new.astype(BF16)
        mk = mk_ref[0].astype(BF16)
        mv = mv_ref[0].astype(BF16)
        yc, ym = [], []
        for s in range(sub // Q_PAIR):
            lo = s * Q_PAIR
            base = pl.multiple_of(row0 + lo, Q_PAIR)
            key_ok = (lax.broadcasted_iota(jnp.int32, (1, K_PAIR), 1) + base) >= ATTN_WINDOW
            qs = _stack_heads(zc[lo:lo + Q_PAIR, 0:G] * QK_SCALE)
            sc = jnp.where(key_ok, _dot_nt(qs, kbuf[pl.ds(base, K_PAIR), :]) + bias_ref[...], NEG_INF)
            (e,), inv = _softmax_parts([sc])
            o = _dot(e.astype(BF16), vbuf[pl.ds(base, K_PAIR), :]) * inv
            yc.append(group_norm(_unstack_heads(o), 2))
            qs = _stack_heads(zc[lo:lo + Q_PAIR, 3 * G:4 * G] * QK_SCALE)
            (e,), inv = _softmax_parts([_dot_nt(qs, mk)])
            o = _dot(e.astype(BF16), mv) * inv
            ym.append(group_norm(_unstack_heads(o), 3))
        return jnp.concatenate(yc, axis=0), jnp.concatenate(ym, axis=0)

    ha = [hist_a[s] for s in range(s_blk)]
    hb = [hist_b[s] for s in range(s_blk)]
    for r0 in range(0, t_blk, sub):
        x = x_ref[:, r0:r0 + sub, :].reshape(rows, D_MODEL)
        h = _rms(x, g1_ref[...]).astype(BF16)
        ya, yb, ha, hb = groups_ab(h, ha, hb)
        yc, ym = groups_cm(h, r0)
        x1 = x + _dot(jnp.concatenate([ya, yb, yc, ym], axis=1), w_out_ref[...])
        x1_ref[:, r0:r0 + sub, :] = x1.reshape(s_blk, sub, D_MODEL)
    for s in range(s_blk):
        new_a_ref[s] = ha[s]
        new_b_ref[s] = hb[s]
        hist_a[s] = ha[s]
        hist_b[s] = hb[s]


def _const_spec(shape):
    zeros = (0,) * len(shape)
    return pl.BlockSpec(shape, lambda b, t: zeros)


def _mixer(x, mk, mv, prev_a, prev_b, cache_k, cache_v, weights, bias_tables, *, prompt):
    (g1, w_in, caw, cbw, cbb, lng, lnb, gng, w_out) = weights
    nb, seq, _ = x.shape
    if prompt:
        s_blk, t_blk = 1, PROMPT_TILE
    else:
        s_blk, t_blk = SAMPLE_STREAMS, seq
    grid = (nb // s_blk, seq // t_blk)
    nt = grid[1]
    rows = s_blk * t_blk

    stream_spec = lambda r, c: pl.BlockSpec((s_blk, r, c), lambda b, t: (b, 0, 0))
    tile_spec = lambda c: pl.BlockSpec((s_blk, t_blk, c), lambda b, t: (b, t, 0))
    in_specs = [tile_spec(D_MODEL), stream_spec(N_MEM, G), stream_spec(N_MEM, G),
                stream_spec(A_HEAD, G), stream_spec(B_HEAD, G)]
    args = [x, mk, mv, prev_a, prev_b]
    if not prompt:
        in_specs += [stream_spec(ATTN_WINDOW, G), stream_spec(ATTN_WINDOW, G)]
        args += [cache_k, cache_v]
    consts = [g1, w_in, caw, cbw, cbb, lng, lnb, *bias_tables, gng, w_out]
    in_specs += [_const_spec(c.shape) for c in consts]
    args += consts

    if prompt:
        first_kept = nt - ATTN_WINDOW // t_blk
        kv_spec = pl.BlockSpec((1, t_blk, G), lambda b, t: (b, jnp.maximum(t - first_kept, 0), 0))
        kv_rows = ATTN_WINDOW
    else:
        kv_spec = tile_spec(G)
        kv_rows = seq
    out_specs = [tile_spec(D_MODEL), stream_spec(A_HEAD, G), stream_spec(B_HEAD, G), kv_spec, kv_spec]
    out_shape = [jax.ShapeDtypeStruct((nb, seq, D_MODEL), F32),
                 jax.ShapeDtypeStruct((nb, A_HEAD, G), F32),
                 jax.ShapeDtypeStruct((nb, B_HEAD, G), F32),
                 jax.ShapeDtypeStruct((nb, kv_rows, G), F32),
                 jax.ShapeDtypeStruct((nb, kv_rows, G), F32)]
    scratch = [pltpu.VMEM((s_blk, A_HEAD, G), F32),
               pltpu.VMEM((s_blk, B_HEAD, G), F32)]
    if prompt:
        scratch += [pltpu.VMEM((ATTN_WINDOW + seq, G), BF16), pltpu.VMEM((ATTN_WINDOW + seq, G), BF16)]
    else:
        scratch += [pltpu.VMEM((rows, 4 * G), F32), pltpu.VMEM((rows, 2 * G), BF16)]
    return pl.pallas_call(
        functools.partial(_mixer_body, prompt=prompt, s_blk=s_blk, t_blk=t_blk),
        grid=grid, in_specs=in_specs, out_specs=out_specs, out_shape=out_shape, scratch_shapes=scratch,
        compiler_params=pltpu.CompilerParams(dimension_semantics=("arbitrary", "arbitrary"),
                                             vmem_limit_bytes=VMEM_LIMIT),
        name="mixer_prompt" if prompt else "mixer_sample",
    )(*args)


def _ffn_body(x_ref, prev_ref, g2_ref, wu_ref, cw_ref, cb_ref, wd_ref, gf_ref,
              out_ref, new_f_ref, carry, *, s_blk, t_blk, final):
    sub = min(t_blk, SUB_TILE)
    rows = s_blk * sub
    t = pl.program_id(1)

    @pl.when(t == 0)
    def _():
        carry[...] = prev_ref[...]

    cols = [c for j in range(N_FF_CHUNKS) for c in (j * FF_CHUNK, D_FF + j * FF_CHUNK)]
    hist = {c: [carry[s, :, c:c + FF_CHUNK] for s in range(s_blk)] for c in cols}
    for r0 in range(0, t_blk, sub):
        x = x_ref[:, r0:r0 + sub, :].reshape(rows, D_MODEL)
        h = _rms(x, g2_ref[...]).astype(BF16)
        acts = []
        for j in range(N_FF_CHUNKS):
            halves = []
            for col in (j * FF_CHUNK, D_FF + j * FF_CHUNK):
                u = _dot(h, wu_ref[:, col:col + FF_CHUNK])
                cw = cw_ref[:, col:col + FF_CHUNK]
                cb = cb_ref[:, col:col + FF_CHUNK]
                per_stream = []
                for s in range(s_blk):
                    us = u[s * sub:(s + 1) * sub]
                    per_stream.append(_conv3(us, hist[col][s], cw) + cb)
                    hist[col][s] = us[sub - A_HEAD:]
                halves.append(per_stream[0] if s_blk == 1 else jnp.concatenate(per_stream, axis=0))
            val, gate = halves
            acts.append((_silu(gate) * val).astype(BF16))
        out = x + _dot(jnp.concatenate(acts, axis=1), wd_ref[...])
        if final:
            out = _rms(out, gf_ref[...])
        out_ref[:, r0:r0 + sub, :] = out.reshape(s_blk, sub, D_MODEL)

    for c in cols:
        for s in range(s_blk):
            new_f_ref[s, :, c:c + FF_CHUNK] = hist[c][s]
            carry[s, :, c:c + FF_CHUNK] = hist[c][s]


def _ffn(x, prev_f, weights, final_g, *, prompt, final):
    (g2, wu, cw, cb, wd) = weights
    nb, seq, _ = x.shape
    if prompt:
        s_blk, t_blk = 1, PROMPT_TILE
    else:
        s_blk, t_blk = SAMPLE_STREAMS, seq
    grid = (nb // s_blk, seq // t_blk)
    rows = s_blk * t_blk
    tile_spec = pl.BlockSpec((s_blk, t_blk, D_MODEL), lambda b, t: (b, t, 0))
    hist_spec = pl.BlockSpec((s_blk, A_HEAD, 2 * D_FF), lambda b, t: (b, 0, 0))
    consts = [g2, wu, cw, cb, wd, final_g]
    return pl.pallas_call(
        functools.partial(_ffn_body, s_blk=s_blk, t_blk=t_blk, final=final),
        grid=grid,
        in_specs=[tile_spec, hist_spec] + [_const_spec(c.shape) for c in consts],
        out_specs=[tile_spec, hist_spec],
        out_shape=[jax.ShapeDtypeStruct((nb, seq, D_MODEL), F32),
                   jax.ShapeDtypeStruct((nb, A_HEAD, 2 * D_FF), F32)],
        scratch_shapes=[pltpu.VMEM((s_blk, A_HEAD, 2 * D_FF), F32)],
        compiler_params=pltpu.CompilerParams(dimension_semantics=("arbitrary", "arbitrary"),
                                             vmem_limit_bytes=VMEM_LIMIT),
        name="ffn_prompt" if prompt else "ffn_sample",
    )(x, prev_f, *consts)


def _rel_bias_toeplitz(rel_bias_l, n_q, n_k):
    lo = ATTN_WINDOW - (n_k - 1) + REL_CLIP
    hi = ATTN_WINDOW + (n_q - 1) + REL_CLIP
    pad_lo, pad_hi = max(0, -lo), max(0, hi - 2 * REL_CLIP)
    ext = jnp.pad(rel_bias_l, ((0, 0), (pad_lo, pad_hi)), mode="edge")[:, lo + pad_lo:hi + pad_lo + 1]
    period = n_q + n_k
    w = jnp.pad(ext[:, ::-1], ((0, 0), (0, 1)))
    flat = jnp.tile(w, (1, n_q))[:, :n_q * (period - 1)]
    return flat.reshape(-1, n_q, period - 1)[:, :, n_q - 1:n_q - 1 + n_k].astype(F32)


def _prompt_bias(rel_bias_l):
    r = np.arange(Q_PAIR)[:, None]
    kk = np.arange(K_PAIR)[None, :] - CHUNK * (r // CHUNK)
    in_band = (kk >= 0) & (kk < ATTN_WINDOW + CHUNK)
    tab = jnp.where(in_band[None], _rel_bias_toeplitz(rel_bias_l, Q_PAIR, K_PAIR), NEG_INF)
    return tab.reshape(N_HEADS * Q_PAIR, K_PAIR)


def _sample_bias(rel_bias_l, t):
    tab = _rel_bias_toeplitz(rel_bias_l, t, ATTN_WINDOW + t).reshape(N_HEADS * t, ATTN_WINDOW + t)
    return tab[:, :ATTN_WINDOW], tab[:, ATTN_WINDOW:]


def _pad_history(prev, head):
    return jnp.pad(prev, ((0, 0), (head - prev.shape[1], 0), (0, 0)))


def kernel(x_prompt, x_sample, cache_conv_a, cache_conv_b, cache_ffn_conv, cache_attn_k, cache_attn_v, cache_mem_k,
           cache_mem_v, mem_prompt, norm1_g, w_in, w_mem_kv, conv_a_w, conv_b_w, conv_b_bias, ln_b_g, ln_b_b, rel_bias,
           grp_norm_g, w_out, norm2_g, w_up, ffn_conv_w, ffn_conv_b, w_down, final_g):
    depth = w_in.shape[0]
    bp, tp, _ = x_prompt.shape
    bs, ts, _ = x_sample.shape
    assert tp % PROMPT_TILE == 0 and PROMPT_TILE % Q_PAIR == 0 and tp >= ATTN_WINDOW
    assert bs % SAMPLE_STREAMS == 0 and ts % SUBLANES == 0 and ts >= B_HEAD
    assert cache_attn_k.shape[2] == ATTN_WINDOW

    w_in_bf = w_in.astype(BF16)
    w_out_bf = w_out.astype(BF16)
    w_up_bf = w_up.astype(BF16)
    w_down_bf = w_down.astype(BF16)
    mem_k_all, mem_v_all = _memkv(mem_prompt, w_mem_kv.astype(BF16))
    final_row = final_g.reshape(1, D_MODEL)

    zero_a = jnp.zeros((bp, A_HEAD, G), F32)
    zero_b = jnp.zeros((bp, B_HEAD, G), F32)
    zero_f = jnp.zeros((bp, A_HEAD, 2 * D_FF), F32)

    xp, xs = x_prompt, x_sample
    outs = {name: [] for name in ("pa", "pb", "pf", "pk", "pv", "sa", "sb", "sf", "sk", "sv")}
    for l in range(depth):
        mixer_w = (norm1_g[l].reshape(1, D_MODEL), w_in_bf[l], conv_a_w[l], conv_b_w[l],
                   conv_b_bias[l].reshape(1, G), ln_b_g[l].reshape(1, G), ln_b_b[l].reshape(1, G),
                   grp_norm_g[l].reshape(1, 4 * G), w_out_bf[l])
        ffn_w = (norm2_g[l].reshape(1, D_MODEL), w_up_bf[l], ffn_conv_w[l],
                 ffn_conv_b[l].reshape(1, 2 * D_FF), w_down_bf[l])
        final = l == depth - 1

        xp, na, nb, k_p, v_p = _mixer(xp, mem_k_all[l], mem_v_all[l], zero_a, zero_b, None, None, mixer_w,
                                      (_prompt_bias(rel_bias[l]),), prompt=True)
        xp, nf = _ffn(xp, zero_f, ffn_w, final_row, prompt=True, final=final)
        outs["pa"].append(na[:, A_HEAD - (CONV_A_WIDTH - 1):])
        outs["pb"].append(nb[:, B_HEAD - (CONV_B_WIDTH - 1):])
        outs["pf"].append(nf[:, A_HEAD - (FFN_CONV_WIDTH - 1):])
        outs["pk"].append(k_p.reshape(bp, ATTN_WINDOW, N_HEADS, HEAD_DIM))
        outs["pv"].append(v_p.reshape(bp, ATTN_WINDOW, N_HEADS, HEAD_DIM))

        xs, na, nb, k_s, v_s = _mixer(xs, cache_mem_k[l].reshape(bs, N_MEM, G), cache_mem_v[l].reshape(bs, N_MEM, G),
                                      _pad_history(cache_conv_a[l], A_HEAD), _pad_history(cache_conv_b[l], B_HEAD),
                                      cache_attn_k[l].reshape(bs, ATTN_WINDOW, G),
                                      cache_attn_v[l].reshape(bs, ATTN_WINDOW, G),
                                      mixer_w, _sample_bias(rel_bias[l], ts), prompt=False)
        xs, nf = _ffn(xs, _pad_history(cache_ffn_conv[l], A_HEAD), ffn_w, final_row, prompt=False, final=final)
        outs["sa"].append(na[:, A_HEAD - (CONV_A_WIDTH - 1):])
        outs["sb"].append(nb[:, B_HEAD - (CONV_B_WIDTH - 1):])
        outs["sf"].append(nf[:, A_HEAD - (FFN_CONV_WIDTH - 1):])
        outs["sk"].append(k_s.reshape(bs, ts, N_HEADS, HEAD_DIM))
        outs["sv"].append(v_s.reshape(bs, ts, N_HEADS, HEAD_DIM))

    st = {name: jnp.stack(v) for name, v in outs.items()}
    mem_shape = (depth, bp, N_MEM, N_HEADS, HEAD_DIM)
    return (xp, xs, st["pa"], st["pb"], st["pf"], st["pk"], st["pv"],
            mem_k_all.reshape(mem_shape), mem_v_all.reshape(mem_shape),
            st["sa"], st["sb"], st["sf"], st["sk"], st["sv"])
```

```python
import functools

import numpy as np
import jax
import jax.numpy as jnp
from jax import lax
from jax.experimental import pallas as pl
from jax.experimental.pallas import tpu as pltpu

F32 = jnp.float32
BF16 = jnp.bfloat16

D_MODEL = 1024
G = 256
N_HEADS = 4
HEAD_DIM = 64
CHUNK = 64
N_MEM = 256
N_PREV_CHUNKS = 8
ATTN_WINDOW = N_PREV_CHUNKS * CHUNK
REL_CLIP = 128
CONV_A_WIDTH = 3
CONV_B_WIDTH = 31
FFN_CONV_WIDTH = 3
D_FF = 2816
D_IN_PROJ = 9 * G
RMS_EPS = 1e-6
LN_EPS = 1e-5
NEG_INF = -1e30
QK_SCALE = HEAD_DIM ** -0.5

SUBLANES = 8
A_HEAD = SUBLANES
B_HEAD = 32
Q_PAIR = 2 * CHUNK
K_PAIR = ATTN_WINDOW + Q_PAIR
FF_CHUNK = 256
N_FF_CHUNKS = D_FF // FF_CHUNK
VMEM_LIMIT = 56 * 1024 * 1024

PROMPT_TILE = 1024
FFN_TILE = 512
SUB_TILE = 256
PROJ_COLS = 256
SAMPLE_STREAMS = 8


def _rms(x, g):
    return x * lax.rsqrt(jnp.mean(x * x, axis=-1, keepdims=True) + RMS_EPS) * g


def _dot(a, b):
    return jnp.dot(a, b, preferred_element_type=F32)


def _dot_nt(a, b):
    return lax.dot_general(a, b, (((1,), (1,)), ((), ())), preferred_element_type=F32)


def _silu(x):
    return x * jax.nn.sigmoid(x)


def _interleave(stages):
    live = list(stages)
    while live:
        for g in list(live):
            try:
                next(g)
            except StopIteration:
                live.remove(g)


def _roll_rows(x, shift):
    shift = shift % x.shape[0]
    return x if shift == 0 else pltpu.roll(x, shift, axis=0)


def _shift_rows(u, hist, s):
    moved = _roll_rows(u, s)
    sub = lax.broadcasted_iota(jnp.int32, (SUBLANES, u.shape[1]), 0)
    top = jnp.where(sub < s, _roll_rows(hist, s), moved[0:SUBLANES])
    return jnp.concatenate([top, moved[SUBLANES:]], axis=0)


def _conv3(u, hist, w):
    return w[2:3] * u + w[1:2] * _shift_rows(u, hist, 1) + w[0:1] * _shift_rows(u, hist, 2)


def _memkv_body(mem_ref, w_ref, mk_ref, mv_ref):
    kv = _dot(mem_ref[0].astype(BF16), w_ref[0])
    mk_ref[0, 0] = kv[:, :G]
    mv_ref[0, 0] = kv[:, G:]


def _memkv(mem_prompt, w_mem_kv_bf):
    depth = w_mem_kv_bf.shape[0]
    b = mem_prompt.shape[0]
    out = jax.ShapeDtypeStruct((depth, b, N_MEM, G), F32)
    return pl.pallas_call(
        _memkv_body,
        grid=(depth, b),
        in_specs=[pl.BlockSpec((1, N_MEM, D_MODEL), lambda l, i: (i, 0, 0)),
                  pl.BlockSpec((1, D_MODEL, 2 * G), lambda l, i: (l, 0, 0))],
        out_specs=[pl.BlockSpec((1, 1, N_MEM, G), lambda l, i: (l, i, 0, 0)),
                   pl.BlockSpec((1, 1, N_MEM, G), lambda l, i: (l, i, 0, 0))],
        out_shape=[out, out],
        compiler_params=pltpu.CompilerParams(dimension_semantics=("arbitrary", "arbitrary")),
        name="memkv",
    )(mem_prompt, w_mem_kv_bf)


def _softmax_parts(s_list):
    m = s_list[0].max(axis=-1, keepdims=True)
    for s in s_list[1:]:
        m = jnp.maximum(m, s.max(axis=-1, keepdims=True))
    e_list = [jnp.exp(s - m) for s in s_list]
    l = e_list[0].sum(axis=-1, keepdims=True)
    for e in e_list[1:]:
        l = l + e.sum(axis=-1, keepdims=True)
    return e_list, 1.0 / l


def _stack_heads(q):
    head = lax.broadcasted_iota(jnp.int32, q.shape, 1) // HEAD_DIM
    return jnp.concatenate([jnp.where(head == hd, q, 0.0) for hd in range(N_HEADS)], axis=0).astype(BF16)


def _unstack_heads(o):
    t = o.shape[0] // N_HEADS
    head = lax.broadcasted_iota(jnp.int32, (t, G), 1) // HEAD_DIM
    out = o[0:t]
    for hd in range(1, N_HEADS):
        out = jnp.where(head == hd, o[hd * t:(hd + 1) * t], out)
    return out


def _mixer_body(*refs, prompt, s_blk, t_blk):
    if prompt:
        (x_ref, mk_ref, mv_ref, prev_a_ref, prev_b_ref, g1_ref, w_in_ref, caw_ref, cbw_ref, cbb_ref,
         lng_ref, lnb_ref, bias_ref, gng_ref, w_out_ref,
         x1_ref, new_a_ref, new_b_ref, k_out_ref, v_out_ref,
         hist_a, hist_b, h_scr, z_scr, y_scr, kbuf, vbuf) = refs
    else:
        (x_ref, mk_ref, mv_ref, prev_a_ref, prev_b_ref, ck_ref, cv_ref, g1_ref, w_in_ref, caw_ref, cbw_ref,
         cbb_ref, lng_ref, lnb_ref, bias_ref, bias2_ref, gng_ref, w_out_ref,
         x1_ref, new_a_ref, new_b_ref, k_out_ref, v_out_ref,
         hist_a, hist_b, h_scr, z_scr, y_scr) = refs
    sub = min(t_blk, SUB_TILE)
    n_sub = t_blk // sub
    n_slots = z_scr.shape[0]
    rows = s_blk * sub
    t = pl.program_id(1)

    def group_norm(y, gi):
        return _rms(y, gng_ref[:, gi * G:(gi + 1) * G]).astype(BF16)

    @pl.when(t == 0)
    def _():
        hist_a[...] = prev_a_ref[...]
        hist_b[...] = prev_b_ref[...]

    ha = [hist_a[s] for s in range(s_blk)]
    hb = [hist_b[s] for s in range(s_blk)]

    def stage_in(i):
        slot, r0 = i % n_slots, i * sub
        x = x_ref[:, r0:r0 + sub, :].reshape(rows, D_MODEL)
        h_scr[slot] = _rms(x, g1_ref[...]).astype(BF16)
        yield
        for c in range(0, D_IN_PROJ, PROJ_COLS):
            z_scr[slot, :, c:c + PROJ_COLS] = _dot(h_scr[slot], w_in_ref[:, c:c + PROJ_COLS])
            yield

    def stage_conv(i):
        slot = i % n_slots
        caw = caw_ref[...]
        for s in range(s_blk):
            sl = slice(s * sub, (s + 1) * sub)
            p = z_scr[slot, sl, G:2 * G] * z_scr[slot, sl, 2 * G:3 * G]
            y_scr[slot, sl, 0:G] = group_norm(z_scr[slot, sl, 0:G] * _conv3(p, ha[s], caw), 0)
            ha[s] = p[sub - A_HEAD:]
        yield
        cbw = cbw_ref[...]
        first = B_HEAD - (CONV_B_WIDTH - 1)
        r_step = min(sub, 64)
        for s in range(s_blk):
            sl = slice(s * sub, (s + 1) * sub)
            glu = z_scr[slot, sl, 3 * G:4 * G] * jax.nn.sigmoid(z_scr[slot, sl, 4 * G:5 * G])
            xs = jnp.concatenate([hb[s], glu], axis=0)
            hb[s] = glu[sub - B_HEAD:]
            moved = [_roll_rows(xs, -b) for b in range(SUBLANES)]
            if s_blk == 1:
                yield
            for q0 in range(0, sub, r_step):
                acc = None
                for j in range(CONV_B_WIDTH):
                    off = first + j
                    lo = q0 + (off // SUBLANES) * SUBLANES
                    term = cbw[j:j + 1] * moved[off % SUBLANES][lo:lo + r_step]
                    acc = term if acc is None else acc + term
                c = acc + cbb_ref[...]
                cc = c - jnp.mean(c, axis=-1, keepdims=True)
                c = cc * lax.rsqrt(jnp.mean(cc * cc, axis=-1, keepdims=True) + LN_EPS) * lng_ref[...] + lnb_ref[...]
                y_scr[slot, s * sub + q0:s * sub + q0 + r_step, G:2 * G] = group_norm(_silu(c), 1)
                if s_blk == 1:
                    yield
            if s_blk > 1 and s % 2 == 1:
                yield

    def stage_attn(i):
        slot, r0 = i % n_slots, i * sub
        k_new = z_scr[slot, :, 6 * G:7 * G]
        v_new = z_scr[slot, :, 7 * G:8 * G]
        kv_off = t_blk - k_out_ref.shape[1]
        if r0 >= kv_off:
            k_out_ref[:, r0 - kv_off:r0 - kv_off + sub, :] = k_new.reshape(s_blk, sub, G)
            v_out_ref[:, r0 - kv_off:r0 - kv_off + sub, :] = v_new.reshape(s_blk, sub, G)
        if not prompt:
            def stream(j, carry):
                r = pl.multiple_of(j * sub, sub)
                qs = _stack_heads(z_scr[slot, pl.ds(r, sub), 5 * G:6 * G] * QK_SCALE)
                kn = z_scr[slot, pl.ds(r, sub), 6 * G:7 * G].astype(BF16)
                vn = z_scr[slot, pl.ds(r, sub), 7 * G:8 * G].astype(BF16)
                s_old = _dot_nt(qs, ck_ref[j].astype(BF16)) + bias_ref[...]
                s_new = _dot_nt(qs, kn) + bias2_ref[...]
                (e_old, e_new), inv = _softmax_parts([s_old, s_new])
                o = (_dot(e_old.astype(BF16), cv_ref[j].astype(BF16)) + _dot(e_new.astype(BF16), vn)) * inv
                y_scr[slot, pl.ds(r, sub), 2 * G:3 * G] = group_norm(_unstack_heads(o), 2)
                qs = _stack_heads(z_scr[slot, pl.ds(r, sub), 8 * G:9 * G] * QK_SCALE)
                (e,), inv = _softmax_parts([_dot_nt(qs, mk_ref[j].astype(BF16))])
                o = _dot(e.astype(BF16), mv_ref[j].astype(BF16)) * inv
                y_scr[slot, pl.ds(r, sub), 3 * G:4 * G] = group_norm(_unstack_heads(o), 3)
                return carry

            lax.fori_loop(0, s_blk, stream, 0)
            yield
            return

        if i == 0:
            @pl.when(t == 0)
            def _():
                kbuf[0:ATTN_WINDOW, :] = jnp.zeros((ATTN_WINDOW, G), BF16)
                vbuf[0:ATTN_WINDOW, :] = jnp.zeros((ATTN_WINDOW, G), BF16)

        row0 = pl.multiple_of(t * t_blk + r0, sub)
        kbuf[pl.ds(ATTN_WINDOW + row0, sub), :] = k_new.astype(BF16)
        vbuf[pl.ds(ATTN_WINDOW + row0, sub), :] = v_new.astype(BF16)
        yield
        for lo in range(0, sub, Q_PAIR):
            base = pl.multiple_of(row0 + lo, Q_PAIR)
            key_ok = (lax.broadcasted_iota(jnp.int32, (1, K_PAIR), 1) + base) >= ATTN_WINDOW
            qs = _stack_heads(z_scr[slot, lo:lo + Q_PAIR, 5 * G:6 * G] * QK_SCALE)
            sc = jnp.where(key_ok, _dot_nt(qs, kbuf[pl.ds(base, K_PAIR), :]) + bias_ref[...], NEG_INF)
            (e,), inv = _softmax_parts([sc])
            yield
            o = _dot(e.astype(BF16), vbuf[pl.ds(base, K_PAIR), :]) * inv
            y_scr[slot, lo:lo + Q_PAIR, 2 * G:3 * G] = group_norm(_unstack_heads(o), 2)
            yield
            qs = _stack_heads(z_scr[slot, lo:lo + Q_PAIR, 8 * G:9 * G] * QK_SCALE)
            (e,), inv = _softmax_parts([_dot_nt(qs, mk_ref[0].astype(BF16))])
            yield
            o = _dot(e.astype(BF16), mv_ref[0].astype(BF16)) * inv
            y_scr[slot, lo:lo + Q_PAIR, 3 * G:4 * G] = group_norm(_unstack_heads(o), 3)
            yield

    def stage_out(i):
        slot, r0 = i % n_slots, i * sub
        for c in range(0, D_MODEL, PROJ_COLS):
            x = x_ref[:, r0:r0 + sub, c:c + PROJ_COLS].reshape(rows, PROJ_COLS)
            x1 = x + _dot(y_scr[slot], w_out_ref[:, c:c + PROJ_COLS])
            x1_ref[:, r0:r0 + sub, c:c + PROJ_COLS] = x1.reshape(s_blk, sub, PROJ_COLS)
            yield

    _interleave([stage_in(0)])
    for i in range(n_sub):
        stages = [stage_conv(i), stage_attn(i)]
        if i + 1 < n_sub:
            stages.append(stage_in(i + 1))
        if i > 0:
            stages.append(stage_out(i - 1))
        _interleave(stages)
    _interleave([stage_out(n_sub - 1)])
    for s in range(s_blk):
        new_a_ref[s] = ha[s]
        new_b_ref[s] = hb[s]
        hist_a[s] = ha[s]
        hist_b[s] = hb[s]


def _const_spec(shape):
    zeros = (0,) * len(shape)
    return pl.BlockSpec(shape, lambda b, t: zeros)


def _layer_spec(layer, shape):
    zeros = (0,) * (len(shape) - 1)
    return pl.BlockSpec((None,) + tuple(shape[1:]), lambda b, t: (layer,) + zeros)


def _mixer(layer, x, mk, mv, prev_a, prev_b, cache_k, cache_v, weights, bias_tables, *, prompt):
    (g1, w_in, caw, cbw, cbb, lng, lnb, gng, w_out) = weights
    nb, seq, _ = x.shape
    if prompt:
        s_blk, t_blk = 1, PROMPT_TILE
    else:
        s_blk, t_blk = SAMPLE_STREAMS, seq
    grid = (nb // s_blk, seq // t_blk)
    nt = grid[1]
    rows = s_blk * t_blk

    stream_spec = lambda r, c: pl.BlockSpec((s_blk, r, c), lambda b, t: (b, 0, 0))
    layer_stream_spec = lambda r, c: pl.BlockSpec((None, s_blk, r, c), lambda b, t: (layer, b, 0, 0))
    tile_spec = lambda c: pl.BlockSpec((s_blk, t_blk, c), lambda b, t: (b, t, 0))
    in_specs = [tile_spec(D_MODEL), layer_stream_spec(N_MEM, G), layer_stream_spec(N_MEM, G),
                layer_stream_spec(A_HEAD, G), layer_stream_spec(B_HEAD, G)]
    args = [x, mk, mv, prev_a, prev_b]
    if not prompt:
        in_specs += [layer_stream_spec(ATTN_WINDOW, G), layer_stream_spec(ATTN_WINDOW, G)]
        args += [cache_k, cache_v]
    consts = [g1, w_in, caw, cbw, cbb, lng, lnb, *bias_tables, gng, w_out]
    in_specs += [_layer_spec(layer, c.shape) for c in consts]
    args += consts

    if prompt:
        kv_blk = min(t_blk, ATTN_WINDOW)
        first_kept = nt - ATTN_WINDOW // kv_blk
        kv_spec = pl.BlockSpec((1, kv_blk, G), lambda b, t: (b, jnp.maximum(t - first_kept, 0), 0))
        kv_rows = ATTN_WINDOW
    else:
        kv_spec = tile_spec(G)
        kv_rows = seq
    out_specs = [tile_spec(D_MODEL), stream_spec(A_HEAD, G), stream_spec(B_HEAD, G), kv_spec, kv_spec]
    out_shape = [jax.ShapeDtypeStruct((nb, seq, D_MODEL), F32),
                 jax.ShapeDtypeStruct((nb, A_HEAD, G), F32),
                 jax.ShapeDtypeStruct((nb, B_HEAD, G), F32),
                 jax.ShapeDtypeStruct((nb, kv_rows, G), F32),
                 jax.ShapeDtypeStruct((nb, kv_rows, G), F32)]
    sub_rows = s_blk * min(t_blk, SUB_TILE)
    n_slots = min(2, rows // sub_rows)
    scratch = [pltpu.VMEM((s_blk, A_HEAD, G), F32),
               pltpu.VMEM((s_blk, B_HEAD, G), F32),
               pltpu.VMEM((n_slots, sub_rows, D_MODEL), BF16),
               pltpu.VMEM((n_slots, sub_rows, D_IN_PROJ), F32),
               pltpu.VMEM((n_slots, sub_rows, 4 * G), BF16)]
    if prompt:
        scratch += [pltpu.VMEM((ATTN_WINDOW + seq, G), BF16), pltpu.VMEM((ATTN_WINDOW + seq, G), BF16)]
    return pl.pallas_call(
        functools.partial(_mixer_body, prompt=prompt, s_blk=s_blk, t_blk=t_blk),
        grid=grid, in_specs=in_specs, out_specs=out_specs, out_shape=out_shape, scratch_shapes=scratch,
        compiler_params=pltpu.CompilerParams(dimension_semantics=("arbitrary", "arbitrary"),
                                             vmem_limit_bytes=VMEM_LIMIT),
        name="mixer_prompt" if prompt else "mixer_sample",
    )(*args)


def _ffn_body(x_ref, prev_ref, g2_ref, wu_ref, cw_ref, cb_ref, wd_ref, gf_ref,
              out_ref, new_f_ref, carry, *, s_blk, t_blk, final):
    sub = min(t_blk, SUB_TILE)
    rows = s_blk * sub
    t = pl.program_id(1)

    @pl.when(t == 0)
    def _():
        carry[...] = prev_ref[...]

    cols = [c for j in range(N_FF_CHUNKS) for c in (j * FF_CHUNK, D_FF + j * FF_CHUNK)]
    hist = {c: [carry[s, :, c:c + FF_CHUNK] for s in range(s_blk)] for c in cols}
    for r0 in range(0, t_blk, sub):
        x = x_ref[:, r0:r0 + sub, :].reshape(rows, D_MODEL)
        h = _rms(x, g2_ref[...]).astype(BF16)
        acts = []
        for j in range(N_FF_CHUNKS):
            halves = []
            for col in (j * FF_CHUNK, D_FF + j * FF_CHUNK):
                u = _dot(h, wu_ref[:, col:col + FF_CHUNK])
                cw = cw_ref[:, col:col + FF_CHUNK]
                cb = cb_ref[:, col:col + FF_CHUNK]
                per_stream = []
                for s in range(s_blk):
                    us = u[s * sub:(s + 1) * sub]
                    per_stream.append(_conv3(us, hist[col][s], cw) + cb)
                    hist[col][s] = us[sub - A_HEAD:]
                halves.append(per_stream[0] if s_blk == 1 else jnp.concatenate(per_stream, axis=0))
            val, gate = halves
            acts.append((_silu(gate) * val).astype(BF16))
        out = x + _dot(jnp.concatenate(acts, axis=1), wd_ref[...])
        if final:
            out = _rms(out, gf_ref[...])
        out_ref[:, r0:r0 + sub, :] = out.reshape(s_blk, sub, D_MODEL)

    for c in cols:
        for s in range(s_blk):
            new_f_ref[s, :, c:c + FF_CHUNK] = hist[c][s]
            carry[s, :, c:c + FF_CHUNK] = hist[c][s]


def _ffn(layer, x, prev_f, weights, final_g, *, prompt, final):
    (g2, wu, cw, cb, wd) = weights
    nb, seq, _ = x.shape
    if prompt:
        s_blk, t_blk = 1, FFN_TILE
    else:
        s_blk, t_blk = SAMPLE_STREAMS, seq
    grid = (nb // s_blk, seq // t_blk)
    tile_spec = pl.BlockSpec((s_blk, t_blk, D_MODEL), lambda b, t: (b, t, 0))
    hist_spec = pl.BlockSpec((s_blk, A_HEAD, 2 * D_FF), lambda b, t: (b, 0, 0))
    prev_spec = pl.BlockSpec((None, s_blk, A_HEAD, 2 * D_FF), lambda b, t: (layer, b, 0, 0))
    consts = [g2, wu, cw, cb, wd]
    return pl.pallas_call(
        functools.partial(_ffn_body, s_blk=s_blk, t_blk=t_blk, final=final),
        grid=grid,
        in_specs=[tile_spec, prev_spec] + [_layer_spec(layer, c.shape) for c in consts] + [_const_spec(final_g.shape)],
        out_specs=[tile_spec, hist_spec],
        out_shape=[jax.ShapeDtypeStruct((nb, seq, D_MODEL), F32),
                   jax.ShapeDtypeStruct((nb, A_HEAD, 2 * D_FF), F32)],
        scratch_shapes=[pltpu.VMEM((s_blk, A_HEAD, 2 * D_FF), F32)],
        compiler_params=pltpu.CompilerParams(dimension_semantics=("arbitrary", "arbitrary"),
                                             vmem_limit_bytes=VMEM_LIMIT),
        name="ffn_prompt" if prompt else "ffn_sample",
    )(x, prev_f, *consts, final_g)


def _rel_bias_toeplitz(rel_bias_l, n_q, n_k):
    lo = ATTN_WINDOW - (n_k - 1) + REL_CLIP
    hi = ATTN_WINDOW + (n_q - 1) + REL_CLIP
    pad_lo, pad_hi = max(0, -lo), max(0, hi - 2 * REL_CLIP)
    ext = jnp.pad(rel_bias_l, ((0, 0), (pad_lo, pad_hi)), mode="edge")[:, lo + pad_lo:hi + pad_lo + 1]
    period = n_q + n_k
    w = jnp.pad(ext[:, ::-1], ((0, 0), (0, 1)))
    flat = jnp.tile(w, (1, n_q))[:, :n_q * (period - 1)]
    return flat.reshape(-1, n_q, period - 1)[:, :, n_q - 1:n_q - 1 + n_k].astype(F32)


def _prompt_bias(rel_bias):
    r = np.arange(Q_PAIR)[:, None]
    kk = np.arange(K_PAIR)[None, :] - CHUNK * (r // CHUNK)
    in_band = (kk >= 0) & (kk < ATTN_WINDOW + CHUNK)
    depth = rel_bias.shape[0]
    tab = _rel_bias_toeplitz(rel_bias.reshape(depth * N_HEADS, -1), Q_PAIR, K_PAIR)
    return jnp.where(in_band[None], tab, NEG_INF).reshape(depth, N_HEADS * Q_PAIR, K_PAIR)


def _sample_bias(rel_bias, t):
    depth = rel_bias.shape[0]
    tab = _rel_bias_toeplitz(rel_bias.reshape(depth * N_HEADS, -1), t, ATTN_WINDOW + t)
    tab = tab.reshape(depth, N_HEADS * t, ATTN_WINDOW + t)
    return tab[:, :, :ATTN_WINDOW], tab[:, :, ATTN_WINDOW:]


def _pad_history(prev, head):
    return jnp.pad(prev, ((0, 0), (0, 0), (head - prev.shape[2], 0), (0, 0)))


def kernel(x_prompt, x_sample, cache_conv_a, cache_conv_b, cache_ffn_conv, cache_attn_k, cache_attn_v, cache_mem_k,
           cache_mem_v, mem_prompt, norm1_g, w_in, w_mem_kv, conv_a_w, conv_b_w, conv_b_bias, ln_b_g, ln_b_b, rel_bias,
           grp_norm_g, w_out, norm2_g, w_up, ffn_conv_w, ffn_conv_b, w_down, final_g):
    depth = w_in.shape[0]
    bp, tp, _ = x_prompt.shape
    bs, ts, _ = x_sample.shape
    assert tp % PROMPT_TILE == 0 and tp % FFN_TILE == 0 and SUB_TILE % Q_PAIR == 0 and tp >= ATTN_WINDOW
    assert bs % SAMPLE_STREAMS == 0 and ts % SUBLANES == 0 and ts >= B_HEAD
    assert cache_attn_k.shape[2] == ATTN_WINDOW

    row = lambda p: p.reshape(depth, 1, -1)
    mixer_w = (row(norm1_g), w_in.astype(BF16), conv_a_w, conv_b_w, row(conv_b_bias), row(ln_b_g), row(ln_b_b),
               row(grp_norm_g), w_out.astype(BF16))
    ffn_w = (row(norm2_g), w_up.astype(BF16), ffn_conv_w, row(ffn_conv_b), w_down.astype(BF16))
    final_row = final_g.reshape(1, D_MODEL)
    mem_k_all, mem_v_all = _memkv(mem_prompt, w_mem_kv.astype(BF16))

    prompt_bias = (_prompt_bias(rel_bias),)
    sample_bias = _sample_bias(rel_bias, ts)
    zero_a = jnp.zeros((depth, bp, A_HEAD, G), F32)
    zero_b = jnp.zeros((depth, bp, B_HEAD, G), F32)
    zero_f = jnp.zeros((depth, bp, A_HEAD, 2 * D_FF), F32)
    hist_a = _pad_history(cache_conv_a, A_HEAD)
    hist_b = _pad_history(cache_conv_b, B_HEAD)
    hist_f = _pad_history(cache_ffn_conv, A_HEAD)
    mem_k_s = cache_mem_k.reshape(depth, bs, N_MEM, G)
    mem_v_s = cache_mem_v.reshape(depth, bs, N_MEM, G)
    attn_k_s = cache_attn_k.reshape(depth, bs, ATTN_WINDOW, G)
    attn_v_s = cache_attn_v.reshape(depth, bs, ATTN_WINDOW, G)

    xp, xs = x_prompt, x_sample
    outs = {name: [] for name in ("pa", "pb", "pf", "pk", "pv", "sa", "sb", "sf", "sk", "sv")}
    for l in range(depth):
        final = l == depth - 1
        xp, na, nb, k_p, v_p = _mixer(l, xp, mem_k_all, mem_v_all, zero_a, zero_b, None, None, mixer_w,
                                      prompt_bias, prompt=True)
        xp, nf = _ffn(l, xp, zero_f, ffn_w, final_row, prompt=True, final=final)
        outs["pa"].append(na[:, A_HEAD - (CONV_A_WIDTH - 1):])
        outs["pb"].append(nb[:, B_HEAD - (CONV_B_WIDTH - 1):])
        outs["pf"].append(nf[:, A_HEAD - (FFN_CONV_WIDTH - 1):])
        outs["pk"].append(k_p.reshape(bp, ATTN_WINDOW, N_HEADS, HEAD_DIM))
        outs["pv"].append(v_p.reshape(bp, ATTN_WINDOW, N_HEADS, HEAD_DIM))

        xs, na, nb, k_s, v_s = _mixer(l, xs, mem_k_s, mem_v_s, hist_a, hist_b, attn_k_s, attn_v_s, mixer_w,
                                      sample_bias, prompt=False)
        xs, nf = _ffn(l, xs, hist_f, ffn_w, final_row, prompt=False, final=final)
        outs["sa"].append(na[:, A_HEAD - (CONV_A_WIDTH - 1):])
        outs["sb"].append(nb[:, B_HEAD - (CONV_B_WIDTH - 1):])
        outs["sf"].append(nf[:, A_HEAD - (FFN_CONV_WIDTH - 1):])
        outs["sk"].append(k_s.reshape(bs, ts, N_HEADS, HEAD_DIM))
        outs["sv"].append(v_s.reshape(bs, ts, N_HEADS, HEAD_DIM))

    st = {name: jnp.stack(v) for name, v in outs.items()}
    mem_shape = (depth, bp, N_MEM, N_HEADS, HEAD_DIM)
    return (xp, xs, st["pa"], st["pb"], st["pf"], st["pk"], st["pv"],
            mem_k_all.reshape(mem_shape), mem_v_all.reshape(mem_shape),
            st["sa"], st["sb"], st["sf"], st["sk"], st["sv"])
```

```python
import functools

import numpy as np
import jax
import jax.numpy as jnp
from jax import lax
from jax.experimental import pallas as pl
from jax.experimental.pallas import tpu as pltpu

F32 = jnp.float32
BF16 = jnp.bfloat16

D_MODEL = 1024
G = 256
N_HEADS = 4
HEAD_DIM = 64
CHUNK = 64
N_MEM = 256
N_PREV_CHUNKS = 8
ATTN_WINDOW = N_PREV_CHUNKS * CHUNK
REL_CLIP = 128
CONV_A_WIDTH = 3
CONV_B_WIDTH = 31
FFN_CONV_WIDTH = 3
D_FF = 2816
D_IN_PROJ = 9 * G
RMS_EPS = 1e-6
LN_EPS = 1e-5
NEG_INF = -1e30
QK_SCALE = HEAD_DIM ** -0.5

SUBLANES = 8
A_HEAD = SUBLANES
B_HEAD = 32
Q_PAIR = 2 * CHUNK
K_PAIR = ATTN_WINDOW + Q_PAIR
FF_CHUNK = 256
N_FF_CHUNKS = D_FF // FF_CHUNK
VMEM_LIMIT = 56 * 1024 * 1024

PROMPT_TILE = 1024
FFN_TILE = 512
SUB_TILE = 256
PROJ_COLS = 256
DOWN_EVERY = 3
SAMPLE_STREAMS = 8


def _rms(x, g):
    return x * lax.rsqrt(jnp.mean(x * x, axis=-1, keepdims=True) + RMS_EPS) * g


def _dot(a, b):
    return jnp.dot(a, b, preferred_element_type=F32)


def _dot_nt(a, b):
    return lax.dot_general(a, b, (((1,), (1,)), ((), ())), preferred_element_type=F32)


def _silu(x):
    return x * jax.nn.sigmoid(x)


def _interleave(stages, every=None):
    every = every or (1,) * len(stages)
    live = list(range(len(stages)))
    rounds = 0
    while live:
        for k in list(live):
            if len(live) == 1 or rounds % every[k] == every[k] - 1:
                try:
                    next(stages[k])
                except StopIteration:
                    live.remove(k)
        rounds += 1


def _roll_rows(x, shift):
    shift = shift % x.shape[0]
    return x if shift == 0 else pltpu.roll(x, shift, axis=0)


def _shift_rows(u, hist, s):
    moved = _roll_rows(u, s)
    sub = lax.broadcasted_iota(jnp.int32, (SUBLANES, u.shape[1]), 0)
    top = jnp.where(sub < s, _roll_rows(hist, s), moved[0:SUBLANES])
    return jnp.concatenate([top, moved[SUBLANES:]], axis=0)


def _conv3(u, hist, w):
    return w[2:3] * u + w[1:2] * _shift_rows(u, hist, 1) + w[0:1] * _shift_rows(u, hist, 2)


def _memkv_body(mem_ref, w_ref, mk_ref, mv_ref):
    kv = _dot(mem_ref[0].astype(BF16), w_ref[0])
    mk_ref[0, 0] = kv[:, :G]
    mv_ref[0, 0] = kv[:, G:]


def _memkv(mem_prompt, w_mem_kv_bf):
    depth = w_mem_kv_bf.shape[0]
    b = mem_prompt.shape[0]
    out = jax.ShapeDtypeStruct((depth, b, N_MEM, G), F32)
    return pl.pallas_call(
        _memkv_body,
        grid=(depth, b),
        in_specs=[pl.BlockSpec((1, N_MEM, D_MODEL), lambda l, i: (i, 0, 0)),
                  pl.BlockSpec((1, D_MODEL, 2 * G), lambda l, i: (l, 0, 0))],
        out_specs=[pl.BlockSpec((1, 1, N_MEM, G), lambda l, i: (l, i, 0, 0)),
                   pl.BlockSpec((1, 1, N_MEM, G), lambda l, i: (l, i, 0, 0))],
        out_shape=[out, out],
        compiler_params=pltpu.CompilerParams(dimension_semantics=("arbitrary", "arbitrary")),
        name="memkv",
    )(mem_prompt, w_mem_kv_bf)


def _softmax_parts(s_list):
    m = s_list[0].max(axis=-1, keepdims=True)
    for s in s_list[1:]:
        m = jnp.maximum(m, s.max(axis=-1, keepdims=True))
    e_list = [jnp.exp(s - m) for s in s_list]
    l = e_list[0].sum(axis=-1, keepdims=True)
    for e in e_list[1:]:
        l = l + e.sum(axis=-1, keepdims=True)
    return e_list, 1.0 / l


def _stack_heads(q):
    head = lax.broadcasted_iota(jnp.int32, q.shape, 1) // HEAD_DIM
    return jnp.concatenate([jnp.where(head == hd, q, 0.0) for hd in range(N_HEADS)], axis=0).astype(BF16)


def _unstack_heads(o):
    t = o.shape[0] // N_HEADS
    head = lax.broadcasted_iota(jnp.int32, (t, G), 1) // HEAD_DIM
    out = o[0:t]
    for hd in range(1, N_HEADS):
        out = jnp.where(head == hd, o[hd * t:(hd + 1) * t], out)
    return out


def _mixer_body(*refs, prompt, s_blk, t_blk):
    if prompt:
        (x_ref, mk_ref, mv_ref, prev_a_ref, prev_b_ref, g1_ref, w_in_ref, caw_ref, cbw_ref, cbb_ref,
         lng_ref, lnb_ref, bias_ref, gng_ref, w_out_ref,
         x1_ref, new_a_ref, new_b_ref, k_out_ref, v_out_ref,
         hist_a, hist_b, h_scr, z_scr, y_scr, kbuf, vbuf) = refs
    else:
        (x_ref, mk_ref, mv_ref, prev_a_ref, prev_b_ref, ck_ref, cv_ref, g1_ref, w_in_ref, caw_ref, cbw_ref,
         cbb_ref, lng_ref, lnb_ref, bias_ref, bias2_ref, gng_ref, w_out_ref,
         x1_ref, new_a_ref, new_b_ref, k_out_ref, v_out_ref,
         hist_a, hist_b, h_scr, z_scr, y_scr) = refs
    sub = min(t_blk, SUB_TILE)
    n_sub = t_blk // sub
    n_slots = z_scr.shape[0]
    rows = s_blk * sub
    t = pl.program_id(1)

    def group_norm(y, gi):
        return _rms(y, gng_ref[:, gi * G:(gi + 1) * G]).astype(BF16)

    @pl.when(t == 0)
    def _():
        hist_a[...] = prev_a_ref[...]
        hist_b[...] = prev_b_ref[...]

    ha = [hist_a[s] for s in range(s_blk)]
    hb = [hist_b[s] for s in range(s_blk)]

    def stage_in(i):
        slot, r0 = i % n_slots, i * sub
        x = x_ref[:, r0:r0 + sub, :].reshape(rows, D_MODEL)
        h_scr[slot] = _rms(x, g1_ref[...]).astype(BF16)
        yield
        for c in range(0, D_IN_PROJ, PROJ_COLS):
            z_scr[slot, :, c:c + PROJ_COLS] = _dot(h_scr[slot], w_in_ref[:, c:c + PROJ_COLS])
            yield

    def stage_conv(i):
        slot = i % n_slots
        caw = caw_ref[...]
        for s in range(s_blk):
            sl = slice(s * sub, (s + 1) * sub)
            p = z_scr[slot, sl, G:2 * G] * z_scr[slot, sl, 2 * G:3 * G]
            y_scr[slot, sl, 0:G] = group_norm(z_scr[slot, sl, 0:G] * _conv3(p, ha[s], caw), 0)
            ha[s] = p[sub - A_HEAD:]
        yield
        cbw = cbw_ref[...]
        first = B_HEAD - (CONV_B_WIDTH - 1)
        r_step = min(sub, 64)
        for s in range(s_blk):
            sl = slice(s * sub, (s + 1) * sub)
            glu = z_scr[slot, sl, 3 * G:4 * G] * jax.nn.sigmoid(z_scr[slot, sl, 4 * G:5 * G])
            xs = jnp.concatenate([hb[s], glu], axis=0)
            hb[s] = glu[sub - B_HEAD:]
            moved = [_roll_rows(xs, -b) for b in range(SUBLANES)]
            if s_blk == 1:
                yield
            for q0 in range(0, sub, r_step):
                acc = None
                for j in range(CONV_B_WIDTH):
                    off = first + j
                    lo = q0 + (off // SUBLANES) * SUBLANES
                    term = cbw[j:j + 1] * moved[off % SUBLANES][lo:lo + r_step]
                    acc = term if acc is None else acc + term
                c = acc + cbb_ref[...]
                cc = c - jnp.mean(c, axis=-1, keepdims=True)
                c = cc * lax.rsqrt(jnp.mean(cc * cc, axis=-1, keepdims=True) + LN_EPS) * lng_ref[...] + lnb_ref[...]
                y_scr[slot, s * sub + q0:s * sub + q0 + r_step, G:2 * G] = group_norm(_silu(c), 1)
                if s_blk == 1:
                    yield
            if s_blk > 1 and s % 2 == 1:
                yield

    def stage_attn(i):
        slot, r0 = i % n_slots, i * sub
        k_new = z_scr[slot, :, 6 * G:7 * G]
        v_new = z_scr[slot, :, 7 * G:8 * G]
        kv_off = t_blk - k_out_ref.shape[1]
        if r0 >= kv_off:
            k_out_ref[:, r0 - kv_off:r0 - kv_off + sub, :] = k_new.reshape(s_blk, sub, G)
            v_out_ref[:, r0 - kv_off:r0 - kv_off + sub, :] = v_new.reshape(s_blk, sub, G)
        if not prompt:
            def stream(j, carry):
                r = pl.multiple_of(j * sub, sub)
                qs = _stack_heads(z_scr[slot, pl.ds(r, sub), 5 * G:6 * G] * QK_SCALE)
                kn = z_scr[slot, pl.ds(r, sub), 6 * G:7 * G].astype(BF16)
                vn = z_scr[slot, pl.ds(r, sub), 7 * G:8 * G].astype(BF16)
                s_old = _dot_nt(qs, ck_ref[j].astype(BF16)) + bias_ref[...]
                s_new = _dot_nt(qs, kn) + bias2_ref[...]
                (e_old, e_new), inv = _softmax_parts([s_old, s_new])
                o = (_dot(e_old.astype(BF16), cv_ref[j].astype(BF16)) + _dot(e_new.astype(BF16), vn)) * inv
                y_scr[slot, pl.ds(r, sub), 2 * G:3 * G] = group_norm(_unstack_heads(o), 2)
                qs = _stack_heads(z_scr[slot, pl.ds(r, sub), 8 * G:9 * G] * QK_SCALE)
                (e,), inv = _softmax_parts([_dot_nt(qs, mk_ref[j].astype(BF16))])
                o = _dot(e.astype(BF16), mv_ref[j].astype(BF16)) * inv
                y_scr[slot, pl.ds(r, sub), 3 * G:4 * G] = group_norm(_unstack_heads(o), 3)
                return carry

            lax.fori_loop(0, s_blk, stream, 0)
            yield
            return

        if i == 0:
            @pl.when(t == 0)
            def _():
                kbuf[0:ATTN_WINDOW, :] = jnp.zeros((ATTN_WINDOW, G), BF16)
                vbuf[0:ATTN_WINDOW, :] = jnp.zeros((ATTN_WINDOW, G), BF16)

        row0 = pl.multiple_of(t * t_blk + r0, sub)
        kbuf[pl.ds(ATTN_WINDOW + row0, sub), :] = k_new.astype(BF16)
        vbuf[pl.ds(ATTN_WINDOW + row0, sub), :] = v_new.astype(BF16)
        yield
        for lo in range(0, sub, Q_PAIR):
            base = pl.multiple_of(row0 + lo, Q_PAIR)
            qs = _stack_heads(z_scr[slot, lo:lo + Q_PAIR, 5 * G:6 * G] * QK_SCALE)
            sc = _dot_nt(qs, kbuf[pl.ds(base, K_PAIR), :]) + bias_ref[...]
            if r0 + lo < ATTN_WINDOW:
                key_ok = (lax.broadcasted_iota(jnp.int32, (1, K_PAIR), 1) + base) >= ATTN_WINDOW
                sc = jnp.where(key_ok, sc, NEG_INF)
            (e,), inv = _softmax_parts([sc])
            yield
            o = _dot(e.astype(BF16), vbuf[pl.ds(base, K_PAIR), :]) * inv
            y_scr[slot, lo:lo + Q_PAIR, 2 * G:3 * G] = group_norm(_unstack_heads(o), 2)
            yield
            qs = _stack_heads(z_scr[slot, lo:lo + Q_PAIR, 8 * G:9 * G] * QK_SCALE)
            (e,), inv = _softmax_parts([_dot_nt(qs, mk_ref[0].astype(BF16))])
            yield
            o = _dot(e.astype(BF16), mv_ref[0].astype(BF16)) * inv
            y_scr[slot, lo:lo + Q_PAIR, 3 * G:4 * G] = group_norm(_unstack_heads(o), 3)
            yield

    def stage_out(i):
        slot, r0 = i % n_slots, i * sub
        for c in range(0, D_MODEL, PROJ_COLS):
            x = x_ref[:, r0:r0 + sub, c:c + PROJ_COLS].reshape(rows, PROJ_COLS)
            x1 = x + _dot(y_scr[slot], w_out_ref[:, c:c + PROJ_COLS])
            x1_ref[:, r0:r0 + sub, c:c + PROJ_COLS] = x1.reshape(s_blk, sub, PROJ_COLS)
            yield

    _interleave([stage_in(0)])
    for i in range(n_sub):
        stages = [stage_conv(i), stage_attn(i)]
        if i + 1 < n_sub:
            stages.append(stage_in(i + 1))
        if i > 0:
            stages.append(stage_out(i - 1))
        _interleave(stages)
    _interleave([stage_out(n_sub - 1)])
    for s in range(s_blk):
        new_a_ref[s] = ha[s]
        new_b_ref[s] = hb[s]
        hist_a[s] = ha[s]
        hist_b[s] = hb[s]


def _const_spec(shape):
    zeros = (0,) * len(shape)
    return pl.BlockSpec(shape, lambda b, t: zeros)


def _layer_spec(layer, shape):
    zeros = (0,) * (len(shape) - 1)
    return pl.BlockSpec((None,) + tuple(shape[1:]), lambda b, t: (layer,) + zeros)


def _mixer(layer, x, mk, mv, prev_a, prev_b, cache_k, cache_v, weights, bias_tables, *, prompt):
    (g1, w_in, caw, cbw, cbb, lng, lnb, gng, w_out) = weights
    nb, seq, _ = x.shape
    if prompt:
        s_blk, t_blk = 1, PROMPT_TILE
    else:
        s_blk, t_blk = SAMPLE_STREAMS, seq
    grid = (nb // s_blk, seq // t_blk)
    nt = grid[1]
    rows = s_blk * t_blk

    stream_spec = lambda r, c: pl.BlockSpec((s_blk, r, c), lambda b, t: (b, 0, 0))
    layer_stream_spec = lambda r, c: pl.BlockSpec((None, s_blk, r, c), lambda b, t: (layer, b, 0, 0))
    tile_spec = lambda c: pl.BlockSpec((s_blk, t_blk, c), lambda b, t: (b, t, 0))
    in_specs = [tile_spec(D_MODEL), layer_stream_spec(N_MEM, G), layer_stream_spec(N_MEM, G),
                layer_stream_spec(A_HEAD, G), layer_stream_spec(B_HEAD, G)]
    args = [x, mk, mv, prev_a, prev_b]
    if not prompt:
        in_specs += [layer_stream_spec(ATTN_WINDOW, G), layer_stream_spec(ATTN_WINDOW, G)]
        args += [cache_k, cache_v]
    consts = [g1, w_in, caw, cbw, cbb, lng, lnb, *bias_tables, gng, w_out]
    in_specs += [_layer_spec(layer, c.shape) for c in consts]
    args += consts

    if prompt:
        kv_blk = min(t_blk, ATTN_WINDOW)
        first_kept = nt - ATTN_WINDOW // kv_blk
        kv_spec = pl.BlockSpec((1, kv_blk, G), lambda b, t: (b, jnp.maximum(t - first_kept, 0), 0))
        kv_rows = ATTN_WINDOW
    else:
        kv_spec = tile_spec(G)
        kv_rows = seq
    out_specs = [tile_spec(D_MODEL), stream_spec(A_HEAD, G), stream_spec(B_HEAD, G), kv_spec, kv_spec]
    out_shape = [jax.ShapeDtypeStruct((nb, seq, D_MODEL), F32),
                 jax.ShapeDtypeStruct((nb, A_HEAD, G), F32),
                 jax.ShapeDtypeStruct((nb, B_HEAD, G), F32),
                 jax.ShapeDtypeStruct((nb, kv_rows, G), F32),
                 jax.ShapeDtypeStruct((nb, kv_rows, G), F32)]
    sub_rows = s_blk * min(t_blk, SUB_TILE)
    n_slots = min(2, rows // sub_rows)
    scratch = [pltpu.VMEM((s_blk, A_HEAD, G), F32),
               pltpu.VMEM((s_blk, B_HEAD, G), F32),
               pltpu.VMEM((n_slots, sub_rows, D_MODEL), BF16),
               pltpu.VMEM((n_slots, sub_rows, D_IN_PROJ), F32),
               pltpu.VMEM((n_slots, sub_rows, 4 * G), BF16)]
    if prompt:
        scratch += [pltpu.VMEM((ATTN_WINDOW + seq, G), BF16), pltpu.VMEM((ATTN_WINDOW + seq, G), BF16)]
    return pl.pallas_call(
        functools.partial(_mixer_body, prompt=prompt, s_blk=s_blk, t_blk=t_blk),
        grid=grid, in_specs=in_specs, out_specs=out_specs, out_shape=out_shape, scratch_shapes=scratch,
        compiler_params=pltpu.CompilerParams(dimension_semantics=("arbitrary", "arbitrary"),
                                             vmem_limit_bytes=VMEM_LIMIT),
        name="mixer_prompt" if prompt else "mixer_sample",
    )(*args)


def _ffn_body(x_ref, prev_ref, g2_ref, wu_ref, cw_ref, cb_ref, wd_ref, gf_ref,
              out_ref, new_f_ref, carry, h_scr, act_scr, *, s_blk, t_blk, final):
    sub = min(t_blk, SUB_TILE)
    n_sub = t_blk // sub
    n_slots = act_scr.shape[0]
    rows = s_blk * sub
    t = pl.program_id(1)

    @pl.when(t == 0)
    def _():
        carry[...] = prev_ref[...]

    cols = [c for j in range(N_FF_CHUNKS) for c in (j * FF_CHUNK, D_FF + j * FF_CHUNK)]
    hist = {c: [carry[s, :, c:c + FF_CHUNK] for s in range(s_blk)] for c in cols}

    def stage_gated(i):
        slot, r0 = i % n_slots, i * sub
        x = x_ref[:, r0:r0 + sub, :].reshape(rows, D_MODEL)
        h_scr[slot] = _rms(x, g2_ref[...]).astype(BF16)
        yield
        for j in range(N_FF_CHUNKS):
            halves = []
            for col in (j * FF_CHUNK, D_FF + j * FF_CHUNK):
                u = _dot(h_scr[slot], wu_ref[:, col:col + FF_CHUNK])
                cw = cw_ref[:, col:col + FF_CHUNK]
                cb = cb_ref[:, col:col + FF_CHUNK]
                per_stream = []
                for s in range(s_blk):
                    us = u[s * sub:(s + 1) * sub]
                    per_stream.append(_conv3(us, hist[col][s], cw) + cb)
                    hist[col][s] = us[sub - A_HEAD:]
                halves.append(per_stream[0] if s_blk == 1 else jnp.concatenate(per_stream, axis=0))
            val, gate = halves
            act_scr[slot, :, j * FF_CHUNK:(j + 1) * FF_CHUNK] = (_silu(gate) * val).astype(BF16)
            yield

    def stage_down(i):
        slot, r0 = i % n_slots, i * sub
        for c in range(0, D_MODEL, PROJ_COLS):
            x = x_ref[:, r0:r0 + sub, c:c + PROJ_COLS].reshape(rows, PROJ_COLS)
            out = x + _dot(act_scr[slot], wd_ref[:, c:c + PROJ_COLS])
            out_ref[:, r0:r0 + sub, c:c + PROJ_COLS] = out.reshape(s_blk, sub, PROJ_COLS)
            yield
        if final:
            out = out_ref[:, r0:r0 + sub, :].reshape(rows, D_MODEL)
            out_ref[:, r0:r0 + sub, :] = _rms(out, gf_ref[...]).reshape(s_blk, sub, D_MODEL)
            yield

    _interleave([stage_gated(0)])
    for i in range(1, n_sub):
        _interleave([stage_gated(i), stage_down(i - 1)], every=(1, DOWN_EVERY))
    _interleave([stage_down(n_sub - 1)])

    for c in cols:
        for s in range(s_blk):
            new_f_ref[s, :, c:c + FF_CHUNK] = hist[c][s]
            carry[s, :, c:c + FF_CHUNK] = hist[c][s]


def _ffn(layer, x, prev_f, weights, final_g, *, prompt, final):
    (g2, wu, cw, cb, wd) = weights
    nb, seq, _ = x.shape
    if prompt:
        s_blk, t_blk = 1, FFN_TILE
    else:
        s_blk, t_blk = SAMPLE_STREAMS, seq
    grid = (nb // s_blk, seq // t_blk)
    sub_rows = s_blk * min(t_blk, SUB_TILE)
    n_slots = min(2, s_blk * t_blk // sub_rows)
    tile_spec = pl.BlockSpec((s_blk, t_blk, D_MODEL), lambda b, t: (b, t, 0))
    hist_spec = pl.BlockSpec((s_blk, A_HEAD, 2 * D_FF), lambda b, t: (b, 0, 0))
    prev_spec = pl.BlockSpec((None, s_blk, A_HEAD, 2 * D_FF), lambda b, t: (layer, b, 0, 0))
    consts = [g2, wu, cw, cb, wd]
    return pl.pallas_call(
        functools.partial(_ffn_body, s_blk=s_blk, t_blk=t_blk, final=final),
        grid=grid,
        in_specs=[tile_spec, prev_spec] + [_layer_spec(layer, c.shape) for c in consts] + [_const_spec(final_g.shape)],
        out_specs=[tile_spec, hist_spec],
        out_shape=[jax.ShapeDtypeStruct((nb, seq, D_MODEL), F32),
                   jax.ShapeDtypeStruct((nb, A_HEAD, 2 * D_FF), F32)],
        scratch_shapes=[pltpu.VMEM((s_blk, A_HEAD, 2 * D_FF), F32),
                        pltpu.VMEM((n_slots, sub_rows, D_MODEL), BF16),
                        pltpu.VMEM((n_slots, sub_rows, D_FF), BF16)],
        compiler_params=pltpu.CompilerParams(dimension_semantics=("arbitrary", "arbitrary"),
                                             vmem_limit_bytes=VMEM_LIMIT),
        name="ffn_prompt" if prompt else "ffn_sample",
    )(x, prev_f, *consts, final_g)


def _rel_bias_toeplitz(rel_bias_l, n_q, n_k):
    lo = ATTN_WINDOW - (n_k - 1) + REL_CLIP
    hi = ATTN_WINDOW + (n_q - 1) + REL_CLIP
    pad_lo, pad_hi = max(0, -lo), max(0, hi - 2 * REL_CLIP)
    ext = jnp.pad(rel_bias_l, ((0, 0), (pad_lo, pad_hi)), mode="edge")[:, lo + pad_lo:hi + pad_lo + 1]
    period = n_q + n_k
    w = jnp.pad(ext[:, ::-1], ((0, 0), (0, 1)))
    flat = jnp.tile(w, (1, n_q))[:, :n_q * (period - 1)]
    return flat.reshape(-1, n_q, period - 1)[:, :, n_q - 1:n_q - 1 + n_k].astype(F32)


def _prompt_bias(rel_bias):
    r = np.arange(Q_PAIR)[:, None]
    kk = np.arange(K_PAIR)[None, :] - CHUNK * (r // CHUNK)
    in_band = (kk >= 0) & (kk < ATTN_WINDOW + CHUNK)
    depth = rel_bias.shape[0]
    tab = _rel_bias_toeplitz(rel_bias.reshape(depth * N_HEADS, -1), Q_PAIR, K_PAIR)
    return jnp.where(in_band[None], tab, NEG_INF).reshape(depth, N_HEADS * Q_PAIR, K_PAIR)


def _sample_bias(rel_bias, t):
    depth = rel_bias.shape[0]
    tab = _rel_bias_toeplitz(rel_bias.reshape(depth * N_HEADS, -1), t, ATTN_WINDOW + t)
    tab = tab.reshape(depth, N_HEADS * t, ATTN_WINDOW + t)
    return tab[:, :, :ATTN_WINDOW], tab[:, :, ATTN_WINDOW:]


def _pad_history(prev, head):
    return jnp.pad(prev, ((0, 0), (0, 0), (head - prev.shape[2], 0), (0, 0)))


def kernel(x_prompt, x_sample, cache_conv_a, cache_conv_b, cache_ffn_conv, cache_attn_k, cache_attn_v, cache_mem_k,
           cache_mem_v, mem_prompt, norm1_g, w_in, w_mem_kv, conv_a_w, conv_b_w, conv_b_bias, ln_b_g, ln_b_b, rel_bias,
           grp_norm_g, w_out, norm2_g, w_up, ffn_conv_w, ffn_conv_b, w_down, final_g):
    depth = w_in.shape[0]
    bp, tp, _ = x_prompt.shape
    bs, ts, _ = x_sample.shape
    assert tp % PROMPT_TILE == 0 and tp % FFN_TILE == 0 and SUB_TILE % Q_PAIR == 0 and tp >= ATTN_WINDOW
    assert bs % SAMPLE_STREAMS == 0 and ts % SUBLANES == 0 and ts >= B_HEAD
    assert cache_attn_k.shape[2] == ATTN_WINDOW

    row = lambda p: p.reshape(depth, 1, -1)
    mixer_w = (row(norm1_g), w_in.astype(BF16), conv_a_w, conv_b_w, row(conv_b_bias), row(ln_b_g), row(ln_b_b),
               row(grp_norm_g), w_out.astype(BF16))
    ffn_w = (row(norm2_g), w_up.astype(BF16), ffn_conv_w, row(ffn_conv_b), w_down.astype(BF16))
    final_row = final_g.reshape(1, D_MODEL)
    mem_k_all, mem_v_all = _memkv(mem_prompt, w_mem_kv.astype(BF16))

    prompt_bias = (_prompt_bias(rel_bias),)
    sample_bias = _sample_bias(rel_bias, ts)
    zero_a = jnp.zeros((depth, bp, A_HEAD, G), F32)
    zero_b = jnp.zeros((depth, bp, B_HEAD, G), F32)
    zero_f = jnp.zeros((depth, bp, A_HEAD, 2 * D_FF), F32)
    hist_a = _pad_history(cache_conv_a, A_HEAD)
    hist_b = _pad_history(cache_conv_b, B_HEAD)
    hist_f = _pad_history(cache_ffn_conv, A_HEAD)
    mem_k_s = cache_mem_k.reshape(depth, bs, N_MEM, G)
    mem_v_s = cache_mem_v.reshape(depth, bs, N_MEM, G)
    attn_k_s = cache_attn_k.reshape(depth, bs, ATTN_WINDOW, G)
    attn_v_s = cache_attn_v.reshape(depth, bs, ATTN_WINDOW, G)

    xp, xs = x_prompt, x_sample
    outs = {name: [] for name in ("pa", "pb", "pf", "pk", "pv", "sa", "sb", "sf", "sk", "sv")}
    for l in range(depth):
        final = l == depth - 1
        xp, na, nb, k_p, v_p = _mixer(l, xp, mem_k_all, mem_v_all, zero_a, zero_b, None, None, mixer_w,
                                      prompt_bias, prompt=True)
        xp, nf = _ffn(l, xp, zero_f, ffn_w, final_row, prompt=True, final=final)
        outs["pa"].append(na[:, A_HEAD - (CONV_A_WIDTH - 1):])
        outs["pb"].append(nb[:, B_HEAD - (CONV_B_WIDTH - 1):])
        outs["pf"].append(nf[:, A_HEAD - (FFN_CONV_WIDTH - 1):])
        outs["pk"].append(k_p.reshape(bp, ATTN_WINDOW, N_HEADS, HEAD_DIM))
        outs["pv"].append(v_p.reshape(bp, ATTN_WINDOW, N_HEADS, HEAD_DIM))

        xs, na, nb, k_s, v_s = _mixer(l, xs, mem_k_s, mem_v_s, hist_a, hist_b, attn_k_s, attn_v_s, mixer_w,
                                      sample_bias, prompt=False)
        xs, nf = _ffn(l, xs, hist_f, ffn_w, final_row, prompt=False, final=final)
        outs["sa"].append(na[:, A_HEAD - (CONV_A_WIDTH - 1):])
        outs["sb"].append(nb[:, B_HEAD - (CONV_B_WIDTH - 1):])
        outs["sf"].append(nf[:, A_HEAD - (FFN_CONV_WIDTH - 1):])
        outs["sk"].append(k_s.reshape(bs, ts, N_HEADS, HEAD_DIM))
        outs["sv"].append(v_s.reshape(bs, ts, N_HEADS, HEAD_DIM))

    st = {name: jnp.stack(v) for name, v in outs.items()}
    mem_shape = (depth, bp, N_MEM, N_HEADS, HEAD_DIM)
    return (xp, xs, st["pa"], st["pb"], st["pf"], st["pk"], st["pv"],
            mem_k_all.reshape(mem_shape), mem_v_all.reshape(mem_shape),
            st["sa"], st["sb"], st["sf"], st["sk"], st["sv"])
```

```python
import functools

import numpy as np
import jax
import jax.numpy as jnp
from jax import lax
from jax.experimental import pallas as pl
from jax.experimental.pallas import tpu as pltpu

F32 = jnp.float32
BF16 = jnp.bfloat16

D_MODEL = 1024
G = 256
N_HEADS = 4
HEAD_DIM = 64
CHUNK = 64
N_MEM = 256
N_PREV_CHUNKS = 8
ATTN_WINDOW = N_PREV_CHUNKS * CHUNK
REL_CLIP = 128
CONV_A_WIDTH = 3
CONV_B_WIDTH = 31
FFN_CONV_WIDTH = 3
D_FF = 2816
D_IN_PROJ = 9 * G
RMS_EPS = 1e-6
LN_EPS = 1e-5
NEG_INF = -1e30
QK_SCALE = HEAD_DIM ** -0.5

SUBLANES = 8
A_HEAD = SUBLANES
B_HEAD = 32
Q_PAIR = 2 * CHUNK
K_PAIR = ATTN_WINDOW + Q_PAIR
FF_CHUNK = 256
N_FF_CHUNKS = D_FF // FF_CHUNK
VMEM_LIMIT = 56 * 1024 * 1024

PROMPT_TILE = 1024
FFN_TILE = 512
SUB_TILE = 256
PROJ_COLS = 256
DOWN_EVERY = 3
SAMPLE_STREAMS = 8


def _rms(x, g):
    return x * lax.rsqrt(jnp.mean(x * x, axis=-1, keepdims=True) + RMS_EPS) * g


def _dot(a, b):
    return jnp.dot(a, b, preferred_element_type=F32)


def _dot_nt(a, b):
    return lax.dot_general(a, b, (((1,), (1,)), ((), ())), preferred_element_type=F32)


def _silu(x):
    return x * jax.nn.sigmoid(x)


def _interleave(stages, every=None):
    every = every or (1,) * len(stages)
    live = list(range(len(stages)))
    rounds = 0
    while live:
        for k in list(live):
            if len(live) == 1 or rounds % every[k] == every[k] - 1:
                try:
                    next(stages[k])
                except StopIteration:
                    live.remove(k)
        rounds += 1


def _roll_rows(x, shift):
    shift = shift % x.shape[0]
    return x if shift == 0 else pltpu.roll(x, shift, axis=0)


def _shift_rows(u, hist, s):
    moved = _roll_rows(u, s)
    sub = lax.broadcasted_iota(jnp.int32, (SUBLANES, u.shape[1]), 0)
    top = jnp.where(sub < s, _roll_rows(hist, s), moved[0:SUBLANES])
    return jnp.concatenate([top, moved[SUBLANES:]], axis=0)


def _conv3(u, hist, w):
    return w[2:3] * u + w[1:2] * _shift_rows(u, hist, 1) + w[0:1] * _shift_rows(u, hist, 2)


def _memkv_body(mem_ref, w_ref, mk_ref, mv_ref):
    kv = _dot(mem_ref[0].astype(BF16), w_ref[0])
    mk_ref[0, 0] = kv[:, :G]
    mv_ref[0, 0] = kv[:, G:]


def _memkv(mem_prompt, w_mem_kv_bf):
    depth = w_mem_kv_bf.shape[0]
    b = mem_prompt.shape[0]
    out = jax.ShapeDtypeStruct((depth, b, N_MEM, G), F32)
    return pl.pallas_call(
        _memkv_body,
        grid=(depth, b),
        in_specs=[pl.BlockSpec((1, N_MEM, D_MODEL), lambda l, i: (i, 0, 0)),
                  pl.BlockSpec((1, D_MODEL, 2 * G), lambda l, i: (l, 0, 0))],
        out_specs=[pl.BlockSpec((1, 1, N_MEM, G), lambda l, i: (l, i, 0, 0)),
                   pl.BlockSpec((1, 1, N_MEM, G), lambda l, i: (l, i, 0, 0))],
        out_shape=[out, out],
        compiler_params=pltpu.CompilerParams(dimension_semantics=("arbitrary", "arbitrary")),
        name="memkv",
    )(mem_prompt, w_mem_kv_bf)


def _softmax_parts(s_list):
    m = s_list[0].max(axis=-1, keepdims=True)
    for s in s_list[1:]:
        m = jnp.maximum(m, s.max(axis=-1, keepdims=True))
    e_list = [jnp.exp(s - m) for s in s_list]
    l = e_list[0].sum(axis=-1, keepdims=True)
    for e in e_list[1:]:
        l = l + e.sum(axis=-1, keepdims=True)
    return e_list, 1.0 / l


def _stack_heads(q):
    head = lax.broadcasted_iota(jnp.int32, q.shape, 1) // HEAD_DIM
    return jnp.concatenate([jnp.where(head == hd, q, 0.0) for hd in range(N_HEADS)], axis=0).astype(BF16)


def _unstack_heads(o):
    t = o.shape[0] // N_HEADS
    head = lax.broadcasted_iota(jnp.int32, (t, G), 1) // HEAD_DIM
    out = o[0:t]
    for hd in range(1, N_HEADS):
        out = jnp.where(head == hd, o[hd * t:(hd + 1) * t], out)
    return out


def _mixer_body(*refs, prompt, s_blk, t_blk):
    if prompt:
        (x_ref, mk_ref, mv_ref, prev_a_ref, prev_b_ref, g1_ref, w_in_ref, caw_ref, cbw_ref, cbb_ref,
         lng_ref, lnb_ref, bias_ref, gng_ref, w_out_ref,
         x1_ref, new_a_ref, new_b_ref, k_out_ref, v_out_ref,
         hist_a, hist_b, h_scr, z_scr, y_scr, kbuf, vbuf) = refs
    else:
        (x_ref, mk_ref, mv_ref, prev_a_ref, prev_b_ref, ck_ref, cv_ref, g1_ref, w_in_ref, caw_ref, cbw_ref,
         cbb_ref, lng_ref, lnb_ref, bias_ref, bias2_ref, gng_ref, w_out_ref,
         x1_ref, new_a_ref, new_b_ref, k_out_ref, v_out_ref,
         hist_a, hist_b, h_scr, z_scr, y_scr) = refs
    sub = min(t_blk, SUB_TILE)
    n_sub = t_blk // sub
    n_slots = z_scr.shape[0]
    rows = s_blk * sub
    t = pl.program_id(1)

    def group_norm(y, gi):
        return _rms(y, gng_ref[:, gi * G:(gi + 1) * G]).astype(BF16)

    @pl.when(t == 0)
    def _():
        hist_a[...] = prev_a_ref[...]
        hist_b[...] = prev_b_ref[...]

    ha = [hist_a[s] for s in range(s_blk)]
    hb = [hist_b[s] for s in range(s_blk)]

    def stage_in(i):
        slot, r0 = i % n_slots, i * sub
        x = x_ref[:, r0:r0 + sub, :].reshape(rows, D_MODEL)
        h_scr[slot] = _rms(x, g1_ref[...]).astype(BF16)
        yield
        for c in range(0, D_IN_PROJ, PROJ_COLS):
            z_scr[slot, :, c:c + PROJ_COLS] = _dot(h_scr[slot], w_in_ref[:, c:c + PROJ_COLS])
            yield

    def stage_conv(i):
        slot = i % n_slots
        caw = caw_ref[...]
        for s in range(s_blk):
            sl = slice(s * sub, (s + 1) * sub)
            p = z_scr[slot, sl, G:2 * G] * z_scr[slot, sl, 2 * G:3 * G]
            y_scr[slot, sl, 0:G] = group_norm(z_scr[slot, sl, 0:G] * _conv3(p, ha[s], caw), 0)
            ha[s] = p[sub - A_HEAD:]
        yield
        cbw = cbw_ref[...]
        first = B_HEAD - (CONV_B_WIDTH - 1)
        r_step = min(sub, 64)
        for s in range(s_blk):
            sl = slice(s * sub, (s + 1) * sub)
            glu = z_scr[slot, sl, 3 * G:4 * G] * jax.nn.sigmoid(z_scr[slot, sl, 4 * G:5 * G])
            xs = jnp.concatenate([hb[s], glu], axis=0)
            hb[s] = glu[sub - B_HEAD:]
            moved = [_roll_rows(xs, -b) for b in range(SUBLANES)]
            if s_blk == 1:
                yield
            for q0 in range(0, sub, r_step):
                acc = None
                for j in range(CONV_B_WIDTH):
                    off = first + j
                    lo = q0 + (off // SUBLANES) * SUBLANES
                    term = cbw[j:j + 1] * moved[off % SUBLANES][lo:lo + r_step]
                    acc = term if acc is None else acc + term
                c = acc + cbb_ref[...]
                cc = c - jnp.mean(c, axis=-1, keepdims=True)
                c = cc * lax.rsqrt(jnp.mean(cc * cc, axis=-1, keepdims=True) + LN_EPS) * lng_ref[...] + lnb_ref[...]
                y_scr[slot, s * sub + q0:s * sub + q0 + r_step, G:2 * G] = group_norm(_silu(c), 1)
                if s_blk == 1:
                    yield
            if s_blk > 1 and s % 2 == 1:
                yield

    def stage_attn(i):
        slot, r0 = i % n_slots, i * sub
        k_new = z_scr[slot, :, 6 * G:7 * G]
        v_new = z_scr[slot, :, 7 * G:8 * G]
        kv_off = t_blk - k_out_ref.shape[1]
        if r0 >= kv_off:
            k_out_ref[:, r0 - kv_off:r0 - kv_off + sub, :] = k_new.reshape(s_blk, sub, G)
            v_out_ref[:, r0 - kv_off:r0 - kv_off + sub, :] = v_new.reshape(s_blk, sub, G)
        if not prompt:
            for j in range(s_blk):
                sl = slice(j * sub, (j + 1) * sub)
                qs = _stack_heads(z_scr[slot, sl, 5 * G:6 * G] * QK_SCALE)
                kn = z_scr[slot, sl, 6 * G:7 * G].astype(BF16)
                vn = z_scr[slot, sl, 7 * G:8 * G].astype(BF16)
                s_old = _dot_nt(qs, ck_ref[j].astype(BF16)) + bias_ref[...]
                s_new = _dot_nt(qs, kn) + bias2_ref[...]
                (e_old, e_new), inv = _softmax_parts([s_old, s_new])
                o = (_dot(e_old.astype(BF16), cv_ref[j].astype(BF16)) + _dot(e_new.astype(BF16), vn)) * inv
                y_scr[slot, sl, 2 * G:3 * G] = group_norm(_unstack_heads(o), 2)
                qs = _stack_heads(z_scr[slot, sl, 8 * G:9 * G] * QK_SCALE)
                (e,), inv = _softmax_parts([_dot_nt(qs, mk_ref[j].astype(BF16))])
                o = _dot(e.astype(BF16), mv_ref[j].astype(BF16)) * inv
                y_scr[slot, sl, 3 * G:4 * G] = group_norm(_unstack_heads(o), 3)
                yield
            return

        if i == 0:
            @pl.when(t == 0)
            def _():
                kbuf[0:ATTN_WINDOW, :] = jnp.zeros((ATTN_WINDOW, G), BF16)
                vbuf[0:ATTN_WINDOW, :] = jnp.zeros((ATTN_WINDOW, G), BF16)

        row0 = pl.multiple_of(t * t_blk + r0, sub)
        kbuf[pl.ds(ATTN_WINDOW + row0, sub), :] = k_new.astype(BF16)
        vbuf[pl.ds(ATTN_WINDOW + row0, sub), :] = v_new.astype(BF16)
        yield
        for lo in range(0, sub, Q_PAIR):
            base = pl.multiple_of(row0 + lo, Q_PAIR)
            qs = _stack_heads(z_scr[slot, lo:lo + Q_PAIR, 5 * G:6 * G] * QK_SCALE)
            sc = _dot_nt(qs, kbuf[pl.ds(base, K_PAIR), :]) + bias_ref[...]
            if r0 + lo < ATTN_WINDOW:
                key_ok = (lax.broadcasted_iota(jnp.int32, (1, K_PAIR), 1) + base) >= ATTN_WINDOW
                sc = jnp.where(key_ok, sc, NEG_INF)
            (e,), inv = _softmax_parts([sc])
            yield
            o = _dot(e.astype(BF16), vbuf[pl.ds(base, K_PAIR), :]) * inv
            y_scr[slot, lo:lo + Q_PAIR, 2 * G:3 * G] = group_norm(_unstack_heads(o), 2)
            yield
            qs = _stack_heads(z_scr[slot, lo:lo + Q_PAIR, 8 * G:9 * G] * QK_SCALE)
            (e,), inv = _softmax_parts([_dot_nt(qs, mk_ref[0].astype(BF16))])
            yield
            o = _dot(e.astype(BF16), mv_ref[0].astype(BF16)) * inv
            y_scr[slot, lo:lo + Q_PAIR, 3 * G:4 * G] = group_norm(_unstack_heads(o), 3)
            yield

    def stage_out(i):
        slot, r0 = i % n_slots, i * sub
        for c in range(0, D_MODEL, PROJ_COLS):
            x = x_ref[:, r0:r0 + sub, c:c + PROJ_COLS].reshape(rows, PROJ_COLS)
            x1 = x + _dot(y_scr[slot], w_out_ref[:, c:c + PROJ_COLS])
            x1_ref[:, r0:r0 + sub, c:c + PROJ_COLS] = x1.reshape(s_blk, sub, PROJ_COLS)
            yield

    _interleave([stage_in(0)])
    for i in range(n_sub):
        stages = [stage_conv(i), stage_attn(i)]
        if i + 1 < n_sub:
            stages.append(stage_in(i + 1))
        if i > 0:
            stages.append(stage_out(i - 1))
        _interleave(stages)
    _interleave([stage_out(n_sub - 1)])
    for s in range(s_blk):
        new_a_ref[s] = ha[s]
        new_b_ref[s] = hb[s]
        hist_a[s] = ha[s]
        hist_b[s] = hb[s]


def _const_spec(shape):
    zeros = (0,) * len(shape)
    return pl.BlockSpec(shape, lambda b, t: zeros)


def _layer_spec(layer, shape):
    zeros = (0,) * (len(shape) - 1)
    return pl.BlockSpec((None,) + tuple(shape[1:]), lambda b, t: (layer,) + zeros, pipeline_mode=pl.Buffered(1))


def _mixer(layer, x, mk, mv, prev_a, prev_b, cache_k, cache_v, weights, bias_tables, *, prompt):
    (g1, w_in, caw, cbw, cbb, lng, lnb, gng, w_out) = weights
    nb, seq, _ = x.shape
    if prompt:
        s_blk, t_blk = 1, PROMPT_TILE
    else:
        s_blk, t_blk = SAMPLE_STREAMS, seq
    grid = (nb // s_blk, seq // t_blk)
    nt = grid[1]
    rows = s_blk * t_blk

    stream_spec = lambda r, c: pl.BlockSpec((s_blk, r, c), lambda b, t: (b, 0, 0))
    layer_stream_spec = lambda r, c: pl.BlockSpec((None, s_blk, r, c), lambda b, t: (layer, b, 0, 0))
    tile_spec = lambda c: pl.BlockSpec((s_blk, t_blk, c), lambda b, t: (b, t, 0))
    in_specs = [tile_spec(D_MODEL), layer_stream_spec(N_MEM, G), layer_stream_spec(N_MEM, G),
                layer_stream_spec(A_HEAD, G), layer_stream_spec(B_HEAD, G)]
    args = [x, mk, mv, prev_a, prev_b]
    if not prompt:
        in_specs += [layer_stream_spec(ATTN_WINDOW, G), layer_stream_spec(ATTN_WINDOW, G)]
        args += [cache_k, cache_v]
    consts = [g1, w_in, caw, cbw, cbb, lng, lnb, *bias_tables, gng, w_out]
    in_specs += [_layer_spec(layer, c.shape) for c in consts]
    args += consts

    if prompt:
        kv_blk = min(t_blk, ATTN_WINDOW)
        first_kept = nt - ATTN_WINDOW // kv_blk
        kv_spec = pl.BlockSpec((1, kv_blk, G), lambda b, t: (b, jnp.maximum(t - first_kept, 0), 0))
        kv_rows = ATTN_WINDOW
    else:
        kv_spec = tile_spec(G)
        kv_rows = seq
    out_specs = [tile_spec(D_MODEL), stream_spec(A_HEAD, G), stream_spec(B_HEAD, G), kv_spec, kv_spec]
    out_shape = [jax.ShapeDtypeStruct((nb, seq, D_MODEL), F32),
                 jax.ShapeDtypeStruct((nb, A_HEAD, G), F32),
                 jax.ShapeDtypeStruct((nb, B_HEAD, G), F32),
                 jax.ShapeDtypeStruct((nb, kv_rows, G), F32),
                 jax.ShapeDtypeStruct((nb, kv_rows, G), F32)]
    sub_rows = s_blk * min(t_blk, SUB_TILE)
    n_slots = min(2, rows // sub_rows)
    scratch = [pltpu.VMEM((s_blk, A_HEAD, G), F32),
               pltpu.VMEM((s_blk, B_HEAD, G), F32),
               pltpu.VMEM((n_slots, sub_rows, D_MODEL), BF16),
               pltpu.VMEM((n_slots, sub_rows, D_IN_PROJ), F32),
               pltpu.VMEM((n_slots, sub_rows, 4 * G), BF16)]
    if prompt:
        scratch += [pltpu.VMEM((ATTN_WINDOW + seq, G), BF16), pltpu.VMEM((ATTN_WINDOW + seq, G), BF16)]
    return pl.pallas_call(
        functools.partial(_mixer_body, prompt=prompt, s_blk=s_blk, t_blk=t_blk),
        grid=grid, in_specs=in_specs, out_specs=out_specs, out_shape=out_shape, scratch_shapes=scratch,
        compiler_params=pltpu.CompilerParams(dimension_semantics=("arbitrary", "arbitrary"),
                                             vmem_limit_bytes=VMEM_LIMIT),
        name="mixer_prompt" if prompt else "mixer_sample",
    )(*args)


def _ffn_body(x_ref, prev_ref, g2_ref, wu_ref, cw_ref, cb_ref, wd_ref, gf_ref,
              out_ref, new_f_ref, carry, h_scr, act_scr, *, s_blk, t_blk, final):
    sub = min(t_blk, SUB_TILE)
    n_sub = t_blk // sub
    n_slots = act_scr.shape[0]
    rows = s_blk * sub
    t = pl.program_id(1)

    @pl.when(t == 0)
    def _():
        carry[...] = prev_ref[...]

    cols = [c for j in range(N_FF_CHUNKS) for c in (j * FF_CHUNK, D_FF + j * FF_CHUNK)]
    hist = {c: [carry[s, :, c:c + FF_CHUNK] for s in range(s_blk)] for c in cols}

    def stage_gated(i):
        slot, r0 = i % n_slots, i * sub
        x = x_ref[:, r0:r0 + sub, :].reshape(rows, D_MODEL)
        h_scr[slot] = _rms(x, g2_ref[...]).astype(BF16)
        yield
        for j in range(N_FF_CHUNKS):
            halves = []
            for col in (j * FF_CHUNK, D_FF + j * FF_CHUNK):
                u = _dot(h_scr[slot], wu_ref[:, col:col + FF_CHUNK])
                cw = cw_ref[:, col:col + FF_CHUNK]
                cb = cb_ref[:, col:col + FF_CHUNK]
                per_stream = []
                for s in range(s_blk):
                    us = u[s * sub:(s + 1) * sub]
                    per_stream.append(_conv3(us, hist[col][s], cw) + cb)
                    hist[col][s] = us[sub - A_HEAD:]
                halves.append(per_stream[0] if s_blk == 1 else jnp.concatenate(per_stream, axis=0))
            val, gate = halves
            act_scr[slot, :, j * FF_CHUNK:(j + 1) * FF_CHUNK] = (_silu(gate) * val).astype(BF16)
            yield

    def stage_down(i):
        slot, r0 = i % n_slots, i * sub
        for c in range(0, D_MODEL, PROJ_COLS):
            x = x_ref[:, r0:r0 + sub, c:c + PROJ_COLS].reshape(rows, PROJ_COLS)
            out = x + _dot(act_scr[slot], wd_ref[:, c:c + PROJ_COLS])
            out_ref[:, r0:r0 + sub, c:c + PROJ_COLS] = out.reshape(s_blk, sub, PROJ_COLS)
            yield
        if final:
            out = out_ref[:, r0:r0 + sub, :].reshape(rows, D_MODEL)
            out_ref[:, r0:r0 + sub, :] = _rms(out, gf_ref[...]).reshape(s_blk, sub, D_MODEL)
            yield

    _interleave([stage_gated(0)])
    for i in range(1, n_sub):
        _interleave([stage_gated(i), stage_down(i - 1)], every=(1, DOWN_EVERY))
    _interleave([stage_down(n_sub - 1)])

    for c in cols:
        for s in range(s_blk):
            new_f_ref[s, :, c:c + FF_CHUNK] = hist[c][s]
            carry[s, :, c:c + FF_CHUNK] = hist[c][s]


def _ffn(layer, x, prev_f, weights, final_g, *, prompt, final):
    (g2, wu, cw, cb, wd) = weights
    nb, seq, _ = x.shape
    if prompt:
        s_blk, t_blk = 1, FFN_TILE
    else:
        s_blk, t_blk = SAMPLE_STREAMS, seq
    grid = (nb // s_blk, seq // t_blk)
    sub_rows = s_blk * min(t_blk, SUB_TILE)
    n_slots = min(2, s_blk * t_blk // sub_rows)
    tile_spec = pl.BlockSpec((s_blk, t_blk, D_MODEL), lambda b, t: (b, t, 0))
    hist_spec = pl.BlockSpec((s_blk, A_HEAD, 2 * D_FF), lambda b, t: (b, 0, 0))
    prev_spec = pl.BlockSpec((None, s_blk, A_HEAD, 2 * D_FF), lambda b, t: (layer, b, 0, 0))
    consts = [g2, wu, cw, cb, wd]
    return pl.pallas_call(
        functools.partial(_ffn_body, s_blk=s_blk, t_blk=t_blk, final=final),
        grid=grid,
        in_specs=[tile_spec, prev_spec] + [_layer_spec(layer, c.shape) for c in consts] + [_const_spec(final_g.shape)],
        out_specs=[tile_spec, hist_spec],
        out_shape=[jax.ShapeDtypeStruct((nb, seq, D_MODEL), F32),
                   jax.ShapeDtypeStruct((nb, A_HEAD, 2 * D_FF), F32)],
        scratch_shapes=[pltpu.VMEM((s_blk, A_HEAD, 2 * D_FF), F32),
                        pltpu.VMEM((n_slots, sub_rows, D_MODEL), BF16),
                        pltpu.VMEM((n_slots, sub_rows, D_FF), BF16)],
        compiler_params=pltpu.CompilerParams(dimension_semantics=("arbitrary", "arbitrary"),
                                             vmem_limit_bytes=VMEM_LIMIT),
        name="ffn_prompt" if prompt else "ffn_sample",
    )(x, prev_f, *consts, final_g)


def _rel_bias_toeplitz(rel_bias_l, n_q, n_k):
    lo = ATTN_WINDOW - (n_k - 1) + REL_CLIP
    hi = ATTN_WINDOW + (n_q - 1) + REL_CLIP
    pad_lo, pad_hi = max(0, -lo), max(0, hi - 2 * REL_CLIP)
    ext = jnp.pad(rel_bias_l, ((0, 0), (pad_lo, pad_hi)), mode="edge")[:, lo + pad_lo:hi + pad_lo + 1]
    period = n_q + n_k
    w = jnp.pad(ext[:, ::-1], ((0, 0), (0, 1)))
    flat = jnp.tile(w, (1, n_q))[:, :n_q * (period - 1)]
    return flat.reshape(-1, n_q, period - 1)[:, :, n_q - 1:n_q - 1 + n_k].astype(F32)


def _prompt_bias(rel_bias):
    r = np.arange(Q_PAIR)[:, None]
    kk = np.arange(K_PAIR)[None, :] - CHUNK * (r // CHUNK)
    in_band = (kk >= 0) & (kk < ATTN_WINDOW + CHUNK)
    depth = rel_bias.shape[0]
    tab = _rel_bias_toeplitz(rel_bias.reshape(depth * N_HEADS, -1), Q_PAIR, K_PAIR)
    return jnp.where(in_band[None], tab, NEG_INF).reshape(depth, N_HEADS * Q_PAIR, K_PAIR)


def _sample_bias(rel_bias, t):
    depth = rel_bias.shape[0]
    tab = _rel_bias_toeplitz(rel_bias.reshape(depth * N_HEADS, -1), t, ATTN_WINDOW + t)
    tab = tab.reshape(depth, N_HEADS * t, ATTN_WINDOW + t)
    return tab[:, :, :ATTN_WINDOW], tab[:, :, ATTN_WINDOW:]


def _pad_history(prev, head):
    return jnp.pad(prev, ((0, 0), (0, 0), (head - prev.shape[2], 0), (0, 0)))


def kernel(x_prompt, x_sample, cache_conv_a, cache_conv_b, cache_ffn_conv, cache_attn_k, cache_attn_v, cache_mem_k,
           cache_mem_v, mem_prompt, norm1_g, w_in, w_mem_kv, conv_a_w, conv_b_w, conv_b_bias, ln_b_g, ln_b_b, rel_bias,
           grp_norm_g, w_out, norm2_g, w_up, ffn_conv_w, ffn_conv_b, w_down, final_g):
    depth = w_in.shape[0]
    bp, tp, _ = x_prompt.shape
    bs, ts, _ = x_sample.shape
    assert tp % PROMPT_TILE == 0 and tp % FFN_TILE == 0 and SUB_TILE % Q_PAIR == 0 and tp >= ATTN_WINDOW
    assert bs % SAMPLE_STREAMS == 0 and ts % SUBLANES == 0 and ts >= B_HEAD
    assert cache_attn_k.shape[2] == ATTN_WINDOW

    row = lambda p: p.reshape(depth, 1, -1)
    mixer_w = (row(norm1_g), w_in.astype(BF16), conv_a_w, conv_b_w, row(conv_b_bias), row(ln_b_g), row(ln_b_b),
               row(grp_norm_g), w_out.astype(BF16))
    ffn_w = (row(norm2_g), w_up.astype(BF16), ffn_conv_w, row(ffn_conv_b), w_down.astype(BF16))
    final_row = final_g.reshape(1, D_MODEL)
    mem_k_all, mem_v_all = _memkv(mem_prompt, w_mem_kv.astype(BF16))

    prompt_bias = (_prompt_bias(rel_bias),)
    sample_bias = _sample_bias(rel_bias, ts)
    zero_a = jnp.zeros((depth, bp, A_HEAD, G), F32)
    zero_b = jnp.zeros((depth, bp, B_HEAD, G), F32)
    zero_f = jnp.zeros((depth, bp, A_HEAD, 2 * D_FF), F32)
    hist_a = _pad_history(cache_conv_a, A_HEAD)
    hist_b = _pad_history(cache_conv_b, B_HEAD)
    hist_f = _pad_history(cache_ffn_conv, A_HEAD)
    mem_k_s = cache_mem_k.reshape(depth, bs, N_MEM, G)
    mem_v_s = cache_mem_v.reshape(depth, bs, N_MEM, G)
    attn_k_s = cache_attn_k.reshape(depth, bs, ATTN_WINDOW, G)
    attn_v_s = cache_attn_v.reshape(depth, bs, ATTN_WINDOW, G)

    xp, xs = x_prompt, x_sample
    outs = {name: [] for name in ("pa", "pb", "pf", "pk", "pv", "sa", "sb", "sf", "sk", "sv")}
    for l in range(depth):
        final = l == depth - 1
        xp, na, nb, k_p, v_p = _mixer(l, xp, mem_k_all, mem_v_all, zero_a, zero_b, None, None, mixer_w,
                                      prompt_bias, prompt=True)
        xp, nf = _ffn(l, xp, zero_f, ffn_w, final_row, prompt=True, final=final)
        outs["pa"].append(na[:, A_HEAD - (CONV_A_WIDTH - 1):])
        outs["pb"].append(nb[:, B_HEAD - (CONV_B_WIDTH - 1):])
        outs["pf"].append(nf[:, A_HEAD - (FFN_CONV_WIDTH - 1):])
        outs["pk"].append(k_p.reshape(bp, ATTN_WINDOW, N_HEADS, HEAD_DIM))
        outs["pv"].append(v_p.reshape(bp, ATTN_WINDOW, N_HEADS, HEAD_DIM))

        xs, na, nb, k_s, v_s = _mixer(l, xs, mem_k_s, mem_v_s, hist_a, hist_b, attn_k_s, attn_v_s, mixer_w,
                                      sample_bias, prompt=False)
        xs, nf = _ffn(l, xs, hist_f, ffn_w, final_row, prompt=False, final=final)
        outs["sa"].append(na[:, A_HEAD - (CONV_A_WIDTH - 1):])
        outs["sb"].append(nb[:, B_HEAD - (CONV_B_WIDTH - 1):])
        outs["sf"].append(nf[:, A_HEAD - (FFN_CONV_WIDTH - 1):])
        outs["sk"].append(k_s.reshape(bs, ts, N_HEADS, HEAD_DIM))
        outs["sv"].append(v_s.reshape(bs, ts, N_HEADS, HEAD_DIM))

    st = {name: jnp.stack(v) for name, v in outs.items()}
    mem_shape = (depth, bp, N_MEM, N_HEADS, HEAD_DIM)
    return (xp, xs, st["pa"], st["pb"], st["pf"], st["pk"], st["pv"],
            mem_k_all.reshape(mem_shape), mem_v_all.reshape(mem_shape),
            st["sa"], st["sb"], st["sf"], st["sk"], st["sv"])
```

```python
import functools

import numpy as np
import jax
import jax.numpy as jnp
from jax import lax
from jax.experimental import pallas as pl
from jax.experimental.pallas import tpu as pltpu

F32 = jnp.float32
BF16 = jnp.bfloat16

D_MODEL = 1024
G = 256
N_HEADS = 4
HEAD_DIM = 64
CHUNK = 64
N_MEM = 256
N_PREV_CHUNKS = 8
ATTN_WINDOW = N_PREV_CHUNKS * CHUNK
REL_CLIP = 128
CONV_A_WIDTH = 3
CONV_B_WIDTH = 31
FFN_CONV_WIDTH = 3
D_FF = 2816
D_IN_PROJ = 9 * G
RMS_EPS = 1e-6
LN_EPS = 1e-5
NEG_INF = -1e30
QK_SCALE = HEAD_DIM ** -0.5

SUBLANES = 8
A_HEAD = SUBLANES
B_HEAD = 32
Q_PAIR = 2 * CHUNK
K_PAIR = ATTN_WINDOW + Q_PAIR
FF_CHUNK = 256
N_FF_CHUNKS = D_FF // FF_CHUNK
VMEM_LIMIT = 56 * 1024 * 1024

PROMPT_TILE = 1024
FFN_TILE = 512
SUB_TILE = 256
PROJ_COLS = 256
DOWN_EVERY = 3
SAMPLE_STREAMS = 8


def _rms(x, g):
    return x * lax.rsqrt(jnp.mean(x * x, axis=-1, keepdims=True) + RMS_EPS) * g


def _dot(a, b):
    return jnp.dot(a, b, preferred_element_type=F32)


def _dot_nt(a, b):
    return lax.dot_general(a, b, (((1,), (1,)), ((), ())), preferred_element_type=F32)


def _silu(x):
    return x * jax.nn.sigmoid(x)


def _interleave(stages, every=None):
    every = every or (1,) * len(stages)
    live = list(range(len(stages)))
    rounds = 0
    while live:
        for k in list(live):
            if len(live) == 1 or rounds % every[k] == every[k] - 1:
                try:
                    next(stages[k])
                except StopIteration:
                    live.remove(k)
        rounds += 1


def _roll_rows(x, shift):
    shift = shift % x.shape[0]
    return x if shift == 0 else pltpu.roll(x, shift, axis=0)


def _shift_rows(u, hist, s):
    moved = _roll_rows(u, s)
    sub = lax.broadcasted_iota(jnp.int32, (SUBLANES, u.shape[1]), 0)
    top = jnp.where(sub < s, _roll_rows(hist, s), moved[0:SUBLANES])
    return jnp.concatenate([top, moved[SUBLANES:]], axis=0)


def _conv3(u, hist, w):
    return w[2:3] * u + w[1:2] * _shift_rows(u, hist, 1) + w[0:1] * _shift_rows(u, hist, 2)


def _memkv_body(mem_ref, w_ref, mk_ref, mv_ref):
    kv = _dot(mem_ref[0].astype(BF16), w_ref[0])
    mk_ref[0, 0] = kv[:, :G]
    mv_ref[0, 0] = kv[:, G:]


def _memkv(mem_prompt, w_mem_kv_bf):
    depth = w_mem_kv_bf.shape[0]
    b = mem_prompt.shape[0]
    out = jax.ShapeDtypeStruct((depth, b, N_MEM, G), F32)
    return pl.pallas_call(
        _memkv_body,
        grid=(depth, b),
        in_specs=[pl.BlockSpec((1, N_MEM, D_MODEL), lambda l, i: (i, 0, 0)),
                  pl.BlockSpec((1, D_MODEL, 2 * G), lambda l, i: (l, 0, 0))],
        out_specs=[pl.BlockSpec((1, 1, N_MEM, G), lambda l, i: (l, i, 0, 0)),
                   pl.BlockSpec((1, 1, N_MEM, G), lambda l, i: (l, i, 0, 0))],
        out_shape=[out, out],
        compiler_params=pltpu.CompilerParams(dimension_semantics=("arbitrary", "arbitrary")),
        name="memkv",
    )(mem_prompt, w_mem_kv_bf)


def _softmax_parts(s_list):
    m = s_list[0].max(axis=-1, keepdims=True)
    for s in s_list[1:]:
        m = jnp.maximum(m, s.max(axis=-1, keepdims=True))
    e_list = [jnp.exp(s - m) for s in s_list]
    l = e_list[0].sum(axis=-1, keepdims=True)
    for e in e_list[1:]:
        l = l + e.sum(axis=-1, keepdims=True)
    return e_list, 1.0 / l


def _stack_heads(q):
    head = lax.broadcasted_iota(jnp.int32, q.shape, 1) // HEAD_DIM
    return jnp.concatenate([jnp.where(head == hd, q, 0.0) for hd in range(N_HEADS)], axis=0).astype(BF16)


def _unstack_heads(o):
    t = o.shape[0] // N_HEADS
    head = lax.broadcasted_iota(jnp.int32, (t, G), 1) // HEAD_DIM
    out = o[0:t]
    for hd in range(1, N_HEADS):
        out = jnp.where(head == hd, o[hd * t:(hd + 1) * t], out)
    return out


def _mixer_body(*refs, prompt, s_blk, t_blk):
    if prompt:
        (x_ref, x_next_ref, mk_ref, mv_ref, prev_a_ref, prev_b_ref, g1_ref, w_in_ref, caw_ref, cbw_ref, cbb_ref,
         lng_ref, lnb_ref, bias_ref, gng_ref, w_out_ref,
         x1_ref, new_a_ref, new_b_ref, k_out_ref, v_out_ref,
         hist_a, hist_b, h_scr, z_scr, y_scr, kbuf, vbuf) = refs
    else:
        x_next_ref = None
        (x_ref, mk_ref, mv_ref, prev_a_ref, prev_b_ref, ck_ref, cv_ref, g1_ref, w_in_ref, caw_ref, cbw_ref,
         cbb_ref, lng_ref, lnb_ref, bias_ref, bias2_ref, gng_ref, w_out_ref,
         x1_ref, new_a_ref, new_b_ref, k_out_ref, v_out_ref,
         hist_a, hist_b, h_scr, z_scr, y_scr) = refs
    sub = min(t_blk, SUB_TILE)
    n_sub = t_blk // sub
    n_slots = z_scr.shape[0]
    rows = s_blk * sub
    t = pl.program_id(1)

    def group_norm(y, gi):
        return _rms(y, gng_ref[:, gi * G:(gi + 1) * G]).astype(BF16)

    @pl.when(t == 0)
    def _():
        hist_a[...] = prev_a_ref[...]
        hist_b[...] = prev_b_ref[...]

    ha = [hist_a[s] for s in range(s_blk)]
    hb = [hist_b[s] for s in range(s_blk)]

    def stage_in(i):
        slot = i % n_slots
        x = x_next_ref[...] if i == n_sub else x_ref[:, i * sub:(i + 1) * sub, :]
        h_scr[slot] = _rms(x.reshape(rows, D_MODEL), g1_ref[...]).astype(BF16)
        yield
        for c in range(0, D_IN_PROJ, PROJ_COLS):
            z_scr[slot, :, c:c + PROJ_COLS] = _dot(h_scr[slot], w_in_ref[:, c:c + PROJ_COLS])
            yield

    def stage_conv(i):
        slot = i % n_slots
        caw = caw_ref[...]
        for s in range(s_blk):
            sl = slice(s * sub, (s + 1) * sub)
            p = z_scr[slot, sl, G:2 * G] * z_scr[slot, sl, 2 * G:3 * G]
            y_scr[slot, sl, 0:G] = group_norm(z_scr[slot, sl, 0:G] * _conv3(p, ha[s], caw), 0)
            ha[s] = p[sub - A_HEAD:]
        yield
        cbw = cbw_ref[...]
        first = B_HEAD - (CONV_B_WIDTH - 1)
        r_step = min(sub, 64)
        for s in range(s_blk):
            sl = slice(s * sub, (s + 1) * sub)
            glu = z_scr[slot, sl, 3 * G:4 * G] * jax.nn.sigmoid(z_scr[slot, sl, 4 * G:5 * G])
            xs = jnp.concatenate([hb[s], glu], axis=0)
            hb[s] = glu[sub - B_HEAD:]
            moved = [_roll_rows(xs, -b) for b in range(SUBLANES)]
            if s_blk == 1:
                yield
            for q0 in range(0, sub, r_step):
                acc = None
                for j in range(CONV_B_WIDTH):
                    off = first + j
                    lo = q0 + (off // SUBLANES) * SUBLANES
                    term = cbw[j:j + 1] * moved[off % SUBLANES][lo:lo + r_step]
                    acc = term if acc is None else acc + term
                c = acc + cbb_ref[...]
                cc = c - jnp.mean(c, axis=-1, keepdims=True)
                c = cc * lax.rsqrt(jnp.mean(cc * cc, axis=-1, keepdims=True) + LN_EPS) * lng_ref[...] + lnb_ref[...]
                y_scr[slot, s * sub + q0:s * sub + q0 + r_step, G:2 * G] = group_norm(_silu(c), 1)
                if s_blk == 1:
                    yield
            if s_blk > 1 and s % 2 == 1:
                yield

    def stage_attn(i):
        slot, r0 = i % n_slots, i * sub
        k_new = z_scr[slot, :, 6 * G:7 * G]
        v_new = z_scr[slot, :, 7 * G:8 * G]
        kv_off = t_blk - k_out_ref.shape[1]
        if r0 >= kv_off:
            k_out_ref[:, r0 - kv_off:r0 - kv_off + sub, :] = k_new.reshape(s_blk, sub, G)
            v_out_ref[:, r0 - kv_off:r0 - kv_off + sub, :] = v_new.reshape(s_blk, sub, G)
        if not prompt:
            for j in range(s_blk):
                sl = slice(j * sub, (j + 1) * sub)
                qs = _stack_heads(z_scr[slot, sl, 5 * G:6 * G] * QK_SCALE)
                kn = z_scr[slot, sl, 6 * G:7 * G].astype(BF16)
                vn = z_scr[slot, sl, 7 * G:8 * G].astype(BF16)
                s_old = _dot_nt(qs, ck_ref[j].astype(BF16)) + bias_ref[...]
                s_new = _dot_nt(qs, kn) + bias2_ref[...]
                (e_old, e_new), inv = _softmax_parts([s_old, s_new])
                o = (_dot(e_old.astype(BF16), cv_ref[j].astype(BF16)) + _dot(e_new.astype(BF16), vn)) * inv
                y_scr[slot, sl, 2 * G:3 * G] = group_norm(_unstack_heads(o), 2)
                qs = _stack_heads(z_scr[slot, sl, 8 * G:9 * G] * QK_SCALE)
                (e,), inv = _softmax_parts([_dot_nt(qs, mk_ref[j].astype(BF16))])
                o = _dot(e.astype(BF16), mv_ref[j].astype(BF16)) * inv
                y_scr[slot, sl, 3 * G:4 * G] = group_norm(_unstack_heads(o), 3)
                yield
            return

        if i == 0:
            @pl.when(t == 0)
            def _():
                kbuf[0:ATTN_WINDOW, :] = jnp.zeros((ATTN_WINDOW, G), BF16)
                vbuf[0:ATTN_WINDOW, :] = jnp.zeros((ATTN_WINDOW, G), BF16)

        row0 = pl.multiple_of(t * t_blk + r0, sub)
        kbuf[pl.ds(ATTN_WINDOW + row0, sub), :] = k_new.astype(BF16)
        vbuf[pl.ds(ATTN_WINDOW + row0, sub), :] = v_new.astype(BF16)
        yield
        for lo in range(0, sub, Q_PAIR):
            base = pl.multiple_of(row0 + lo, Q_PAIR)
            qs = _stack_heads(z_scr[slot, lo:lo + Q_PAIR, 5 * G:6 * G] * QK_SCALE)
            sc = _dot_nt(qs, kbuf[pl.ds(base, K_PAIR), :]) + bias_ref[...]
            if r0 + lo < ATTN_WINDOW:
                key_ok = (lax.broadcasted_iota(jnp.int32, (1, K_PAIR), 1) + base) >= ATTN_WINDOW
                sc = jnp.where(key_ok, sc, NEG_INF)
            (e,), inv = _softmax_parts([sc])
            yield
            o = _dot(e.astype(BF16), vbuf[pl.ds(base, K_PAIR), :]) * inv
            y_scr[slot, lo:lo + Q_PAIR, 2 * G:3 * G] = group_norm(_unstack_heads(o), 2)
            yield
            qs = _stack_heads(z_scr[slot, lo:lo + Q_PAIR, 8 * G:9 * G] * QK_SCALE)
            (e,), inv = _softmax_parts([_dot_nt(qs, mk_ref[0].astype(BF16))])
            yield
            o = _dot(e.astype(BF16), mv_ref[0].astype(BF16)) * inv
            y_scr[slot, lo:lo + Q_PAIR, 3 * G:4 * G] = group_norm(_unstack_heads(o), 3)
            yield

    def stage_out(i):
        slot, r0 = i % n_slots, i * sub
        for c in range(0, D_MODEL, PROJ_COLS):
            x = x_ref[:, r0:r0 + sub, c:c + PROJ_COLS].reshape(rows, PROJ_COLS)
            x1 = x + _dot(y_scr[slot], w_out_ref[:, c:c + PROJ_COLS])
            x1_ref[:, r0:r0 + sub, c:c + PROJ_COLS] = x1.reshape(s_blk, sub, PROJ_COLS)
            yield

    carry_over = x_next_ref is not None
    if carry_over:
        @pl.when((pl.program_id(0) == 0) & (t == 0))
        def _():
            _interleave([stage_in(0)])
    else:
        _interleave([stage_in(0)])
    for i in range(n_sub):
        stages = [stage_conv(i), stage_attn(i)]
        if i + 1 < n_sub or carry_over:
            stages.append(stage_in(i + 1))
        if i > 0:
            stages.append(stage_out(i - 1))
        _interleave(stages)
    _interleave([stage_out(n_sub - 1)])
    for s in range(s_blk):
        new_a_ref[s] = ha[s]
        new_b_ref[s] = hb[s]
        hist_a[s] = ha[s]
        hist_b[s] = hb[s]


def _const_spec(shape):
    zeros = (0,) * len(shape)
    return pl.BlockSpec(shape, lambda b, t: zeros)


def _layer_spec(layer, shape):
    zeros = (0,) * (len(shape) - 1)
    return pl.BlockSpec((None,) + tuple(shape[1:]), lambda b, t: (layer,) + zeros, pipeline_mode=pl.Buffered(1))


def _mixer(layer, x, mk, mv, prev_a, prev_b, cache_k, cache_v, weights, bias_tables, *, prompt):
    (g1, w_in, caw, cbw, cbb, lng, lnb, gng, w_out) = weights
    nb, seq, _ = x.shape
    if prompt:
        s_blk, t_blk = 1, PROMPT_TILE
    else:
        s_blk, t_blk = SAMPLE_STREAMS, seq
    grid = (nb // s_blk, seq // t_blk)
    nt = grid[1]
    rows = s_blk * t_blk

    stream_spec = lambda r, c: pl.BlockSpec((s_blk, r, c), lambda b, t: (b, 0, 0))
    layer_stream_spec = lambda r, c: pl.BlockSpec((None, s_blk, r, c), lambda b, t: (layer, b, 0, 0))
    tile_spec = lambda c: pl.BlockSpec((s_blk, t_blk, c), lambda b, t: (b, t, 0))
    in_specs = [tile_spec(D_MODEL), layer_stream_spec(N_MEM, G), layer_stream_spec(N_MEM, G),
                layer_stream_spec(A_HEAD, G), layer_stream_spec(B_HEAD, G)]
    args = [x, mk, mv, prev_a, prev_b]
    if prompt:
        sub = min(t_blk, SUB_TILE)
        assert (t_blk // sub) % 2 == 0 and s_blk == 1
        last = grid[0] * nt - 1

        def next_first_rows(b, t):
            n = jnp.minimum(b * nt + t + 1, last)
            return (n // nt, (n % nt) * (t_blk // sub), 0)

        in_specs.insert(1, pl.BlockSpec((s_blk, sub, D_MODEL), next_first_rows))
        args.insert(1, x)
    if not prompt:
        in_specs += [layer_stream_spec(ATTN_WINDOW, G), layer_stream_spec(ATTN_WINDOW, G)]
        args += [cache_k, cache_v]
    consts = [g1, w_in, caw, cbw, cbb, lng, lnb, *bias_tables, gng, w_out]
    in_specs += [_layer_spec(layer, c.shape) for c in consts]
    args += consts

    if prompt:
        kv_blk = min(t_blk, ATTN_WINDOW)
        first_kept = nt - ATTN_WINDOW // kv_blk
        kv_spec = pl.BlockSpec((1, kv_blk, G), lambda b, t: (b, jnp.maximum(t - first_kept, 0), 0))
        kv_rows = ATTN_WINDOW
    else:
        kv_spec = tile_spec(G)
        kv_rows = seq
    out_specs = [tile_spec(D_MODEL), stream_spec(A_HEAD, G), stream_spec(B_HEAD, G), kv_spec, kv_spec]
    out_shape = [jax.ShapeDtypeStruct((nb, seq, D_MODEL), F32),
                 jax.ShapeDtypeStruct((nb, A_HEAD, G), F32),
                 jax.ShapeDtypeStruct((nb, B_HEAD, G), F32),
                 jax.ShapeDtypeStruct((nb, kv_rows, G), F32),
                 jax.ShapeDtypeStruct((nb, kv_rows, G), F32)]
    sub_rows = s_blk * min(t_blk, SUB_TILE)
    n_slots = min(2, rows // sub_rows)
    scratch = [pltpu.VMEM((s_blk, A_HEAD, G), F32),
               pltpu.VMEM((s_blk, B_HEAD, G), F32),
               pltpu.VMEM((n_slots, sub_rows, D_MODEL), BF16),
               pltpu.VMEM((n_slots, sub_rows, D_IN_PROJ), F32),
               pltpu.VMEM((n_slots, sub_rows, 4 * G), BF16)]
    if prompt:
        scratch += [pltpu.VMEM((ATTN_WINDOW + seq, G), BF16), pltpu.VMEM((ATTN_WINDOW + seq, G), BF16)]
    return pl.pallas_call(
        functools.partial(_mixer_body, prompt=prompt, s_blk=s_blk, t_blk=t_blk),
        grid=grid, in_specs=in_specs, out_specs=out_specs, out_shape=out_shape, scratch_shapes=scratch,
        compiler_params=pltpu.CompilerParams(dimension_semantics=("arbitrary", "arbitrary"),
                                             vmem_limit_bytes=VMEM_LIMIT),
        name="mixer_prompt" if prompt else "mixer_sample",
    )(*args)


def _ffn_body(x_ref, prev_ref, g2_ref, wu_ref, cw_ref, cb_ref, wd_ref, gf_ref,
              out_ref, new_f_ref, carry, h_scr, act_scr, *, s_blk, t_blk, final):
    sub = min(t_blk, SUB_TILE)
    n_sub = t_blk // sub
    n_slots = act_scr.shape[0]
    rows = s_blk * sub
    t = pl.program_id(1)

    @pl.when(t == 0)
    def _():
        carry[...] = prev_ref[...]

    cols = [c for j in range(N_FF_CHUNKS) for c in (j * FF_CHUNK, D_FF + j * FF_CHUNK)]
    hist = {c: [carry[s, :, c:c + FF_CHUNK] for s in range(s_blk)] for c in cols}

    def stage_gated(i):
        slot, r0 = i % n_slots, i * sub
        x = x_ref[:, r0:r0 + sub, :].reshape(rows, D_MODEL)
        h_scr[slot] = _rms(x, g2_ref[...]).astype(BF16)
        yield
        for j in range(N_FF_CHUNKS):
            halves = []
            for col in (j * FF_CHUNK, D_FF + j * FF_CHUNK):
                u = _dot(h_scr[slot], wu_ref[:, col:col + FF_CHUNK])
                cw = cw_ref[:, col:col + FF_CHUNK]
                cb = cb_ref[:, col:col + FF_CHUNK]
                per_stream = []
                for s in range(s_blk):
                    us = u[s * sub:(s + 1) * sub]
                    per_stream.append(_conv3(us, hist[col][s], cw) + cb)
                    hist[col][s] = us[sub - A_HEAD:]
                halves.append(per_stream[0] if s_blk == 1 else jnp.concatenate(per_stream, axis=0))
            val, gate = halves
            act_scr[slot, :, j * FF_CHUNK:(j + 1) * FF_CHUNK] = (_silu(gate) * val).astype(BF16)
            yield

    def stage_down(i):
        slot, r0 = i % n_slots, i * sub
        for c in range(0, D_MODEL, PROJ_COLS):
            x = x_ref[:, r0:r0 + sub, c:c + PROJ_COLS].reshape(rows, PROJ_COLS)
            out = x + _dot(act_scr[slot], wd_ref[:, c:c + PROJ_COLS])
            out_ref[:, r0:r0 + sub, c:c + PROJ_COLS] = out.reshape(s_blk, sub, PROJ_COLS)
            yield
        if final:
            out = out_ref[:, r0:r0 + sub, :].reshape(rows, D_MODEL)
            out_ref[:, r0:r0 + sub, :] = _rms(out, gf_ref[...]).reshape(s_blk, sub, D_MODEL)
            yield

    _interleave([stage_gated(0)])
    for i in range(1, n_sub):
        _interleave([stage_gated(i), stage_down(i - 1)], every=(1, DOWN_EVERY))
    _interleave([stage_down(n_sub - 1)])

    for c in cols:
        for s in range(s_blk):
            new_f_ref[s, :, c:c + FF_CHUNK] = hist[c][s]
            carry[s, :, c:c + FF_CHUNK] = hist[c][s]


def _ffn(layer, x, prev_f, weights, final_g, *, prompt, final):
    (g2, wu, cw, cb, wd) = weights
    nb, seq, _ = x.shape
    if prompt:
        s_blk, t_blk = 1, FFN_TILE
    else:
        s_blk, t_blk = SAMPLE_STREAMS, seq
    grid = (nb // s_blk, seq // t_blk)
    sub_rows = s_blk * min(t_blk, SUB_TILE)
    n_slots = min(2, s_blk * t_blk // sub_rows)
    tile_spec = pl.BlockSpec((s_blk, t_blk, D_MODEL), lambda b, t: (b, t, 0))
    hist_spec = pl.BlockSpec((s_blk, A_HEAD, 2 * D_FF), lambda b, t: (b, 0, 0))
    prev_spec = pl.BlockSpec((None, s_blk, A_HEAD, 2 * D_FF), lambda b, t: (layer, b, 0, 0))
    consts = [g2, wu, cw, cb, wd]
    return pl.pallas_call(
        functools.partial(_ffn_body, s_blk=s_blk, t_blk=t_blk, final=final),
        grid=grid,
        in_specs=[tile_spec, prev_spec] + [_layer_spec(layer, c.shape) for c in consts] + [_const_spec(final_g.shape)],
        out_specs=[tile_spec, hist_spec],
        out_shape=[jax.ShapeDtypeStruct((nb, seq, D_MODEL), F32),
                   jax.ShapeDtypeStruct((nb, A_HEAD, 2 * D_FF), F32)],
        scratch_shapes=[pltpu.VMEM((s_blk, A_HEAD, 2 * D_FF), F32),
                        pltpu.VMEM((n_slots, sub_rows, D_MODEL), BF16),
                        pltpu.VMEM((n_slots, sub_rows, D_FF), BF16)],
        compiler_params=pltpu.CompilerParams(dimension_semantics=("arbitrary", "arbitrary"),
                                             vmem_limit_bytes=VMEM_LIMIT),
        name="ffn_prompt" if prompt else "ffn_sample",
    )(x, prev_f, *consts, final_g)


def _rel_bias_toeplitz(rel_bias_l, n_q, n_k):
    lo = ATTN_WINDOW - (n_k - 1) + REL_CLIP
    hi = ATTN_WINDOW + (n_q - 1) + REL_CLIP
    pad_lo, pad_hi = max(0, -lo), max(0, hi - 2 * REL_CLIP)
    ext = jnp.pad(rel_bias_l, ((0, 0), (pad_lo, pad_hi)), mode="edge")[:, lo + pad_lo:hi + pad_lo + 1]
    period = n_q + n_k
    w = jnp.pad(ext[:, ::-1], ((0, 0), (0, 1)))
    flat = jnp.tile(w, (1, n_q))[:, :n_q * (period - 1)]
    return flat.reshape(-1, n_q, period - 1)[:, :, n_q - 1:n_q - 1 + n_k].astype(F32)


def _prompt_bias(rel_bias):
    r = np.arange(Q_PAIR)[:, None]
    kk = np.arange(K_PAIR)[None, :] - CHUNK * (r // CHUNK)
    in_band = (kk >= 0) & (kk < ATTN_WINDOW + CHUNK)
    depth = rel_bias.shape[0]
    tab = _rel_bias_toeplitz(rel_bias.reshape(depth * N_HEADS, -1), Q_PAIR, K_PAIR)
    return jnp.where(in_band[None], tab, NEG_INF).reshape(depth, N_HEADS * Q_PAIR, K_PAIR)


def _sample_bias(rel_bias, t):
    depth = rel_bias.shape[0]
    tab = _rel_bias_toeplitz(rel_bias.reshape(depth * N_HEADS, -1), t, ATTN_WINDOW + t)
    tab = tab.reshape(depth, N_HEADS * t, ATTN_WINDOW + t)
    return tab[:, :, :ATTN_WINDOW], tab[:, :, ATTN_WINDOW:]


def _pad_history(prev, head):
    return jnp.pad(prev, ((0, 0), (0, 0), (head - prev.shape[2], 0), (0, 0)))


def kernel(x_prompt, x_sample, cache_conv_a, cache_conv_b, cache_ffn_conv, cache_attn_k, cache_attn_v, cache_mem_k,
           cache_mem_v, mem_prompt, norm1_g, w_in, w_mem_kv, conv_a_w, conv_b_w, conv_b_bias, ln_b_g, ln_b_b, rel_bias,
           grp_norm_g, w_out, norm2_g, w_up, ffn_conv_w, ffn_conv_b, w_down, final_g):
    depth = w_in.shape[0]
    bp, tp, _ = x_prompt.shape
    bs, ts, _ = x_sample.shape
    assert tp % PROMPT_TILE == 0 and tp % FFN_TILE == 0 and SUB_TILE % Q_PAIR == 0 and tp >= ATTN_WINDOW
    assert bs % SAMPLE_STREAMS == 0 and ts % SUBLANES == 0 and ts >= B_HEAD
    assert cache_attn_k.shape[2] == ATTN_WINDOW

    row = lambda p: p.reshape(depth, 1, -1)
    mixer_w = (row(norm1_g), w_in.astype(BF16), conv_a_w, conv_b_w, row(conv_b_bias), row(ln_b_g), row(ln_b_b),
               row(grp_norm_g), w_out.astype(BF16))
    ffn_w = (row(norm2_g), w_up.astype(BF16), ffn_conv_w, row(ffn_conv_b), w_down.astype(BF16))
    final_row = final_g.reshape(1, D_MODEL)
    mem_k_all, mem_v_all = _memkv(mem_prompt, w_mem_kv.astype(BF16))

    prompt_bias = (_prompt_bias(rel_bias),)
    sample_bias = _sample_bias(rel_bias, ts)
    zero_a = jnp.zeros((depth, bp, A_HEAD, G), F32)
    zero_b = jnp.zeros((depth, bp, B_HEAD, G), F32)
    zero_f = jnp.zeros((depth, bp, A_HEAD, 2 * D_FF), F32)
    hist_a = _pad_history(cache_conv_a, A_HEAD)
    hist_b = _pad_history(cache_conv_b, B_HEAD)
    hist_f = _pad_history(cache_ffn_conv, A_HEAD)
    mem_k_s = cache_mem_k.reshape(depth, bs, N_MEM, G)
    mem_v_s = cache_mem_v.reshape(depth, bs, N_MEM, G)
    attn_k_s = cache_attn_k.reshape(depth, bs, ATTN_WINDOW, G)
    attn_v_s = cache_attn_v.reshape(depth, bs, ATTN_WINDOW, G)

    xp, xs = x_prompt, x_sample
    outs = {name: [] for name in ("pa", "pb", "pf", "pk", "pv", "sa", "sb", "sf", "sk", "sv")}
    for l in range(depth):
        final = l == depth - 1
        xp, na, nb, k_p, v_p = _mixer(l, xp, mem_k_all, mem_v_all, zero_a, zero_b, None, None, mixer_w,
                                      prompt_bias, prompt=True)
        xp, nf = _ffn(l, xp, zero_f, ffn_w, final_row, prompt=True, final=final)
        outs["pa"].append(na[:, A_HEAD - (CONV_A_WIDTH - 1):])
        outs["pb"].append(nb[:, B_HEAD - (CONV_B_WIDTH - 1):])
        outs["pf"].append(nf[:, A_HEAD - (FFN_CONV_WIDTH - 1):])
        outs["pk"].append(k_p.reshape(bp, ATTN_WINDOW, N_HEADS, HEAD_DIM))
        outs["pv"].append(v_p.reshape(bp, ATTN_WINDOW, N_HEADS, HEAD_DIM))

        xs, na, nb, k_s, v_s = _mixer(l, xs, mem_k_s, mem_v_s, hist_a, hist_b, attn_k_s, attn_v_s, mixer_w,
                                      sample_bias, prompt=False)
        xs, nf = _ffn(l, xs, hist_f, ffn_w, final_row, prompt=False, final=final)
        outs["sa"].append(na[:, A_HEAD - (CONV_A_WIDTH - 1):])
        outs["sb"].append(nb[:, B_HEAD - (CONV_B_WIDTH - 1):])
        outs["sf"].append(nf[:, A_HEAD - (FFN_CONV_WIDTH - 1):])
        outs["sk"].append(k_s.reshape(bs, ts, N_HEADS, HEAD_DIM))
        outs["sv"].append(v_s.reshape(bs, ts, N_HEADS, HEAD_DIM))

    st = {name: jnp.stack(v) for name, v in outs.items()}
    mem_shape = (depth, bp, N_MEM, N_HEADS, HEAD_DIM)
    return (xp, xs, st["pa"], st["pb"], st["pf"], st["pk"], st["pv"],
            mem_k_all.reshape(mem_shape), mem_v_all.reshape(mem_shape),
            st["sa"], st["sb"], st["sf"], st["sk"], st["sv"])
```

```python
import functools

import numpy as np
import jax
import jax.numpy as jnp
from jax import lax
from jax.experimental import pallas as pl
from jax.experimental.pallas import tpu as pltpu

F32 = jnp.float32
BF16 = jnp.bfloat16

D_MODEL = 1024
G = 256
N_HEADS = 4
HEAD_DIM = 64
CHUNK = 64
N_MEM = 256
N_PREV_CHUNKS = 8
ATTN_WINDOW = N_PREV_CHUNKS * CHUNK
REL_CLIP = 128
CONV_A_WIDTH = 3
CONV_B_WIDTH = 31
FFN_CONV_WIDTH = 3
D_FF = 2816
D_IN_PROJ = 9 * G
RMS_EPS = 1e-6
LN_EPS = 1e-5
NEG_INF = -1e30
QK_SCALE = HEAD_DIM ** -0.5
LOG2E = 1.4426950408889634
Q_SCALE = QK_SCALE * LOG2E

SUBLANES = 8
A_HEAD = SUBLANES
B_HEAD = 32
Q_PAIR = 2 * CHUNK
K_PAIR = ATTN_WINDOW + Q_PAIR
FF_CHUNK = 256
N_FF_CHUNKS = D_FF // FF_CHUNK
VMEM_LIMIT = 56 * 1024 * 1024

PROMPT_TILE = 1024
FFN_TILE = 512
SUB_TILE = 256
PROJ_COLS = 256
DOWN_EVERY = 3
SAMPLE_STREAMS = 8
MEM_STREAMS = 4


def _rms(x, g):
    return x * lax.rsqrt(jnp.mean(x * x, axis=-1, keepdims=True) + RMS_EPS) * g


def _dot(a, b):
    return jnp.dot(a, b, preferred_element_type=F32)


def _dot_nt(a, b):
    return lax.dot_general(a, b, (((1,), (1,)), ((), ())), preferred_element_type=F32)


def _silu(x):
    return x * jax.nn.sigmoid(x)


def _interleave(stages, every=None):
    every = every or (1,) * len(stages)
    live = list(range(len(stages)))
    rounds = 0
    while live:
        for k in list(live):
            if len(live) == 1 or rounds % every[k] == every[k] - 1:
                try:
                    next(stages[k])
                except StopIteration:
                    live.remove(k)
        rounds += 1


def _roll_rows(x, shift):
    shift = shift % x.shape[0]
    return x if shift == 0 else pltpu.roll(x, shift, axis=0)


def _shift_rows(u, hist, s):
    moved = _roll_rows(u, s)
    sub = lax.broadcasted_iota(jnp.int32, (SUBLANES, u.shape[1]), 0)
    top = jnp.where(sub < s, _roll_rows(hist, s), moved[0:SUBLANES])
    return jnp.concatenate([top, moved[SUBLANES:]], axis=0)


def _conv3(u, hist, w):
    return w[2:3] * u + w[1:2] * _shift_rows(u, hist, 1) + w[0:1] * _shift_rows(u, hist, 2)


def _memkv_body(mem_ref, w_ref, mk_ref, mv_ref):
    nb = mem_ref.shape[0]
    kv = _dot(mem_ref[...].reshape(nb * N_MEM, D_MODEL).astype(BF16), w_ref[0]).reshape(nb, N_MEM, 2 * G)
    mk_ref[0] = kv[:, :, :G]
    mv_ref[0] = kv[:, :, G:]


def _memkv(mem_prompt, w_mem_kv_bf):
    depth = w_mem_kv_bf.shape[0]
    b = mem_prompt.shape[0]
    nb = MEM_STREAMS if b % MEM_STREAMS == 0 else 1
    out = jax.ShapeDtypeStruct((depth, b, N_MEM, G), F32)
    return pl.pallas_call(
        _memkv_body,
        grid=(b // nb, depth),
        in_specs=[pl.BlockSpec((nb, N_MEM, D_MODEL), lambda i, l: (i, 0, 0)),
                  pl.BlockSpec((1, D_MODEL, 2 * G), lambda i, l: (l, 0, 0))],
        out_specs=[pl.BlockSpec((1, nb, N_MEM, G), lambda i, l: (l, i, 0, 0)),
                   pl.BlockSpec((1, nb, N_MEM, G), lambda i, l: (l, i, 0, 0))],
        out_shape=[out, out],
        compiler_params=pltpu.CompilerParams(dimension_semantics=("arbitrary", "arbitrary"),
                                             vmem_limit_bytes=VMEM_LIMIT),
        name="memkv",
    )(mem_prompt, w_mem_kv_bf)


def _softmax_parts(s_list):
    m = s_list[0].max(axis=-1, keepdims=True)
    for s in s_list[1:]:
        m = jnp.maximum(m, s.max(axis=-1, keepdims=True))
    e_list = [jnp.exp2(s - m) for s in s_list]
    l = e_list[0].sum(axis=-1, keepdims=True)
    for e in e_list[1:]:
        l = l + e.sum(axis=-1, keepdims=True)
    return e_list, 1.0 / l


def _stack_heads(q):
    head = lax.broadcasted_iota(jnp.int32, q.shape, 1) // HEAD_DIM
    return jnp.concatenate([jnp.where(head == hd, q, 0.0) for hd in range(N_HEADS)], axis=0).astype(BF16)


def _unstack_heads(o):
    t = o.shape[0] // N_HEADS
    head = lax.broadcasted_iota(jnp.int32, (t, G), 1) // HEAD_DIM
    out = o[0:t]
    for hd in range(1, N_HEADS):
        out = jnp.where(head == hd, o[hd * t:(hd + 1) * t], out)
    return out


def _mixer_body(*refs, prompt, s_blk, t_blk, n_slots):
    if prompt:
        (x_ref, mk_ref, mv_ref, prev_a_ref, prev_b_ref, g1_ref, w_in_ref, caw_ref, cbw_ref, cbb_ref,
         lng_ref, lnb_ref, bias_ref, gng_ref, w_out_ref,
         x1_ref, new_a_ref, new_b_ref, k_out_ref, v_out_ref,
         hist_a, hist_b, *slots, kbuf, vbuf) = refs
    else:
        (x_ref, mk_ref, mv_ref, prev_a_ref, prev_b_ref, ck_ref, cv_ref, g1_ref, w_in_ref, caw_ref, cbw_ref,
         cbb_ref, lng_ref, lnb_ref, bias_ref, bias2_ref, gng_ref, w_out_ref,
         x1_ref, new_a_ref, new_b_ref, k_out_ref, v_out_ref,
         hist_a, hist_b, *slots) = refs
    h_scr, z_scr, y_scr = slots[:n_slots], slots[n_slots:2 * n_slots], slots[2 * n_slots:]
    sub = min(t_blk, SUB_TILE)
    n_sub = t_blk // sub
    rows = s_blk * sub
    t = pl.program_id(1)

    def group_norm(y, gi):
        return _rms(y, gng_ref[:, gi * G:(gi + 1) * G]).astype(BF16)

    @pl.when(t == 0)
    def _():
        hist_a[...] = prev_a_ref[...]
        hist_b[...] = prev_b_ref[...]

    ha = [hist_a[s] for s in range(s_blk)]
    hb = [hist_b[s] for s in range(s_blk)]

    def stage_in(i):
        slot, r0 = i % n_slots, i * sub
        x = x_ref[:, r0:r0 + sub, :].reshape(rows, D_MODEL)
        h_scr[slot][...] = _rms(x, g1_ref[...]).astype(BF16)
        yield
        for c in range(0, D_IN_PROJ, PROJ_COLS):
            z_scr[slot][:, c:c + PROJ_COLS] = _dot(h_scr[slot][...],w_in_ref[:, c:c + PROJ_COLS])
            yield

    def stage_conv(i):
        slot = i % n_slots
        caw = caw_ref[...]
        for s in range(s_blk):
            sl = slice(s * sub, (s + 1) * sub)
            p = z_scr[slot][sl, G:2 * G] * z_scr[slot][sl, 2 * G:3 * G]
            y_scr[slot][sl, 0:G] = group_norm(z_scr[slot][sl, 0:G] * _conv3(p, ha[s], caw), 0)
            ha[s] = p[sub - A_HEAD:]
        yield
        cbw = cbw_ref[...]
        first = B_HEAD - (CONV_B_WIDTH - 1)
        r_step = min(sub, 64)
        for s in range(s_blk):
            sl = slice(s * sub, (s + 1) * sub)
            glu = z_scr[slot][sl, 3 * G:4 * G] * jax.nn.sigmoid(z_scr[slot][sl, 4 * G:5 * G])
            xs = jnp.concatenate([hb[s], glu], axis=0)
            hb[s] = glu[sub - B_HEAD:]
            moved = [_roll_rows(xs, -b) for b in range(SUBLANES)]
            if s_blk == 1:
                yield
            for q0 in range(0, sub, r_step):
                acc = None
                for j in range(CONV_B_WIDTH):
                    off = first + j
                    lo = q0 + (off // SUBLANES) * SUBLANES
                    term = cbw[j:j + 1] * moved[off % SUBLANES][lo:lo + r_step]
                    acc = term if acc is None else acc + term
                c = acc + cbb_ref[...]
                cc = c - jnp.mean(c, axis=-1, keepdims=True)
                c = cc * lax.rsqrt(jnp.mean(cc * cc, axis=-1, keepdims=True) + LN_EPS) * lng_ref[...] + lnb_ref[...]
                y_scr[slot][s * sub + q0:s * sub + q0 + r_step, G:2 * G] = group_norm(_silu(c), 1)
                if s_blk == 1:
                    yield
            if s_blk > 1 and s % 2 == 1:
                yield

    def stage_attn(i):
        slot, r0 = i % n_slots, i * sub
        k_new = z_scr[slot][:, 6 * G:7 * G]
        v_new = z_scr[slot][:, 7 * G:8 * G]
        kv_off = t_blk - k_out_ref.shape[1]
        if r0 >= kv_off:
            k_out_ref[:, r0 - kv_off:r0 - kv_off + sub, :] = k_new.reshape(s_blk, sub, G)
            v_out_ref[:, r0 - kv_off:r0 - kv_off + sub, :] = v_new.reshape(s_blk, sub, G)
        if not prompt:
            for j in range(s_blk):
                sl = slice(j * sub, (j + 1) * sub)
                qs = _stack_heads(z_scr[slot][sl, 5 * G:6 * G] * Q_SCALE)
                kn = z_scr[slot][sl, 6 * G:7 * G].astype(BF16)
                vn = z_scr[slot][sl, 7 * G:8 * G].astype(BF16)
                s_old = _dot_nt(qs, ck_ref[j].astype(BF16)) + bias_ref[...]
                s_new = _dot_nt(qs, kn) + bias2_ref[...]
                (e_old, e_new), inv = _softmax_parts([s_old, s_new])
                o = (_dot(e_old.astype(BF16), cv_ref[j].astype(BF16)) + _dot(e_new.astype(BF16), vn)) * inv
                y_scr[slot][sl, 2 * G:3 * G] = group_norm(_unstack_heads(o), 2)
                qs = _stack_heads(z_scr[slot][sl, 8 * G:9 * G] * Q_SCALE)
                (e,), inv = _softmax_parts([_dot_nt(qs, mk_ref[j].astype(BF16))])
                o = _dot(e.astype(BF16), mv_ref[j].astype(BF16)) * inv
                y_scr[slot][sl, 3 * G:4 * G] = group_norm(_unstack_heads(o), 3)
                yield
            return

        if i == 0:
            @pl.when(t == 0)
            def _():
                kbuf[0:ATTN_WINDOW, :] = jnp.zeros((ATTN_WINDOW, G), BF16)
                vbuf[0:ATTN_WINDOW, :] = jnp.zeros((ATTN_WINDOW, G), BF16)

        row0 = pl.multiple_of(t * t_blk + r0, sub)
        kbuf[pl.ds(ATTN_WINDOW + row0, sub), :] = k_new.astype(BF16)
        vbuf[pl.ds(ATTN_WINDOW + row0, sub), :] = v_new.astype(BF16)
        yield
        for lo in range(0, sub, Q_PAIR):
            base = pl.multiple_of(row0 + lo, Q_PAIR)
            qs = _stack_heads(z_scr[slot][lo:lo + Q_PAIR, 5 * G:6 * G] * Q_SCALE)
            sc = _dot_nt(qs, kbuf[pl.ds(base, K_PAIR), :]) + bias_ref[...]
            if r0 + lo < ATTN_WINDOW:
                key_ok = (lax.broadcasted_iota(jnp.int32, (1, K_PAIR), 1) + base) >= ATTN_WINDOW
                sc = jnp.where(key_ok, sc, NEG_INF)
            (e,), inv = _softmax_parts([sc])
            yield
            o = _dot(e.astype(BF16), vbuf[pl.ds(base, K_PAIR), :]) * inv
            y_scr[slot][lo:lo + Q_PAIR, 2 * G:3 * G] = group_norm(_unstack_heads(o), 2)
            yield
            qs = _stack_heads(z_scr[slot][lo:lo + Q_PAIR, 8 * G:9 * G] * Q_SCALE)
            (e,), inv = _softmax_parts([_dot_nt(qs, mk_ref[0].astype(BF16))])
            yield
            o = _dot(e.astype(BF16), mv_ref[0].astype(BF16)) * inv
            y_scr[slot][lo:lo + Q_PAIR, 3 * G:4 * G] = group_norm(_unstack_heads(o), 3)
            yield

    def stage_out(i):
        slot, r0 = i % n_slots, i * sub
        for c in range(0, D_MODEL, PROJ_COLS):
            x = x_ref[:, r0:r0 + sub, c:c + PROJ_COLS].reshape(rows, PROJ_COLS)
            x1 = x + _dot(y_scr[slot][...],w_out_ref[:, c:c + PROJ_COLS])
            x1_ref[:, r0:r0 + sub, c:c + PROJ_COLS] = x1.reshape(s_blk, sub, PROJ_COLS)
            yield

    _interleave([stage_in(0)])
    for i in range(n_sub):
        stages = [stage_conv(i), stage_attn(i)]
        if i + 1 < n_sub:
            stages.append(stage_in(i + 1))
        if i > 0:
            stages.append(stage_out(i - 1))
        _interleave(stages)
    _interleave([stage_out(n_sub - 1)])
    for s in range(s_blk):
        new_a_ref[s] = ha[s]
        new_b_ref[s] = hb[s]
        hist_a[s] = ha[s]
        hist_b[s] = hb[s]


def _const_spec(shape):
    zeros = (0,) * len(shape)
    return pl.BlockSpec(shape, lambda b, t: zeros)


def _layer_spec(layer, shape):
    zeros = (0,) * (len(shape) - 1)
    return pl.BlockSpec((None,) + tuple(shape[1:]), lambda b, t: (layer,) + zeros, pipeline_mode=pl.Buffered(1))


def _mixer(layer, x, mk, mv, prev_a, prev_b, cache_k, cache_v, weights, bias_tables, *, prompt):
    (g1, w_in, caw, cbw, cbb, lng, lnb, gng, w_out) = weights
    nb, seq, _ = x.shape
    if prompt:
        s_blk, t_blk = 1, PROMPT_TILE
    else:
        s_blk, t_blk = SAMPLE_STREAMS, seq
    grid = (nb // s_blk, seq // t_blk)
    nt = grid[1]
    rows = s_blk * t_blk

    stream_spec = lambda r, c: pl.BlockSpec((s_blk, r, c), lambda b, t: (b, 0, 0))
    layer_stream_spec = lambda r, c: pl.BlockSpec((None, s_blk, r, c), lambda b, t: (layer, b, 0, 0))
    tile_spec = lambda c: pl.BlockSpec((s_blk, t_blk, c), lambda b, t: (b, t, 0))
    in_specs = [tile_spec(D_MODEL), layer_stream_spec(N_MEM, G), layer_stream_spec(N_MEM, G),
                layer_stream_spec(A_HEAD, G), layer_stream_spec(B_HEAD, G)]
    args = [x, mk, mv, prev_a, prev_b]
    if not prompt:
        in_specs += [layer_stream_spec(ATTN_WINDOW, G), layer_stream_spec(ATTN_WINDOW, G)]
        args += [cache_k, cache_v]
    consts = [g1, w_in, caw, cbw, cbb, lng, lnb, *bias_tables, gng, w_out]
    in_specs += [_layer_spec(layer, c.shape) for c in consts]
    args += consts

    if prompt:
        kv_blk = min(t_blk, ATTN_WINDOW)
        first_kept = nt - ATTN_WINDOW // kv_blk
        kv_spec = pl.BlockSpec((1, kv_blk, G), lambda b, t: (b, jnp.maximum(t - first_kept, 0), 0))
        kv_rows = ATTN_WINDOW
    else:
        kv_spec = tile_spec(G)
        kv_rows = seq
    out_specs = [tile_spec(D_MODEL), stream_spec(A_HEAD, G), stream_spec(B_HEAD, G), kv_spec, kv_spec]
    out_shape = [jax.ShapeDtypeStruct((nb, seq, D_MODEL), F32),
                 jax.ShapeDtypeStruct((nb, A_HEAD, G), F32),
                 jax.ShapeDtypeStruct((nb, B_HEAD, G), F32),
                 jax.ShapeDtypeStruct((nb, kv_rows, G), F32),
                 jax.ShapeDtypeStruct((nb, kv_rows, G), F32)]
    sub_rows = s_blk * min(t_blk, SUB_TILE)
    n_slots = min(2, rows // sub_rows)
    scratch = ([pltpu.VMEM((s_blk, A_HEAD, G), F32), pltpu.VMEM((s_blk, B_HEAD, G), F32)]
               + [pltpu.VMEM((sub_rows, D_MODEL), BF16)] * n_slots
               + [pltpu.VMEM((sub_rows, D_IN_PROJ), F32)] * n_slots
               + [pltpu.VMEM((sub_rows, 4 * G), BF16)] * n_slots)
    if prompt:
        scratch += [pltpu.VMEM((ATTN_WINDOW + seq, G), BF16), pltpu.VMEM((ATTN_WINDOW + seq, G), BF16)]
    return pl.pallas_call(
        functools.partial(_mixer_body, prompt=prompt, s_blk=s_blk, t_blk=t_blk, n_slots=n_slots),
        grid=grid, in_specs=in_specs, out_specs=out_specs, out_shape=out_shape, scratch_shapes=scratch,
        compiler_params=pltpu.CompilerParams(dimension_semantics=("arbitrary", "arbitrary"),
                                             vmem_limit_bytes=VMEM_LIMIT),
        name="mixer_prompt" if prompt else "mixer_sample",
    )(*args)


def _ffn_body(x_ref, prev_ref, g2_ref, wu_ref, cw_ref, cb_ref, wd_ref, gf_ref,
              out_ref, new_f_ref, carry, *slots, s_blk, t_blk, final):
    sub = min(t_blk, SUB_TILE)
    n_sub = t_blk // sub
    n_slots = len(slots) // 2
    h_scr, act_scr = slots[:n_slots], slots[n_slots:]
    rows = s_blk * sub
    t = pl.program_id(1)

    @pl.when(t == 0)
    def _():
        carry[...] = prev_ref[...]

    cols = [c for j in range(N_FF_CHUNKS) for c in (j * FF_CHUNK, D_FF + j * FF_CHUNK)]
    hist = {c: [carry[s, :, c:c + FF_CHUNK] for s in range(s_blk)] for c in cols}

    def stage_gated(i):
        slot, r0 = i % n_slots, i * sub
        x = x_ref[:, r0:r0 + sub, :].reshape(rows, D_MODEL)
        h_scr[slot][...] = _rms(x, g2_ref[...]).astype(BF16)
        yield
        for j in range(N_FF_CHUNKS):
            halves = []
            for col in (j * FF_CHUNK, D_FF + j * FF_CHUNK):
                u = _dot(h_scr[slot][...],wu_ref[:, col:col + FF_CHUNK])
                cw = cw_ref[:, col:col + FF_CHUNK]
                cb = cb_ref[:, col:col + FF_CHUNK]
                per_stream = []
                for s in range(s_blk):
                    us = u[s * sub:(s + 1) * sub]
                    per_stream.append(_conv3(us, hist[col][s], cw) + cb)
                    hist[col][s] = us[sub - A_HEAD:]
                halves.append(per_stream[0] if s_blk == 1 else jnp.concatenate(per_stream, axis=0))
            val, gate = halves
            act_scr[slot][:, j * FF_CHUNK:(j + 1) * FF_CHUNK] = (_silu(gate) * val).astype(BF16)
            yield

    def stage_down(i):
        slot, r0 = i % n_slots, i * sub
        for c in range(0, D_MODEL, PROJ_COLS):
            x = x_ref[:, r0:r0 + sub, c:c + PROJ_COLS].reshape(rows, PROJ_COLS)
            out = x + _dot(act_scr[slot][...],wd_ref[:, c:c + PROJ_COLS])
            out_ref[:, r0:r0 + sub, c:c + PROJ_COLS] = out.reshape(s_blk, sub, PROJ_COLS)
            yield
        if final:
            out = out_ref[:, r0:r0 + sub, :].reshape(rows, D_MODEL)
            out_ref[:, r0:r0 + sub, :] = _rms(out, gf_ref[...]).reshape(s_blk, sub, D_MODEL)
            yield

    _interleave([stage_gated(0)])
    for i in range(1, n_sub):
        _interleave([stage_gated(i), stage_down(i - 1)], every=(1, DOWN_EVERY))
    _interleave([stage_down(n_sub - 1)])

    for c in cols:
        for s in range(s_blk):
            new_f_ref[s, :, c:c + FF_CHUNK] = hist[c][s]
            carry[s, :, c:c + FF_CHUNK] = hist[c][s]


def _ffn(layer, x, prev_f, weights, final_g, *, prompt, final):
    (g2, wu, cw, cb, wd) = weights
    nb, seq, _ = x.shape
    if prompt:
        s_blk, t_blk = 1, FFN_TILE
    else:
        s_blk, t_blk = SAMPLE_STREAMS, seq
    grid = (nb // s_blk, seq // t_blk)
    sub_rows = s_blk * min(t_blk, SUB_TILE)
    n_slots = min(2, s_blk * t_blk // sub_rows)
    tile_spec = pl.BlockSpec((s_blk, t_blk, D_MODEL), lambda b, t: (b, t, 0))
    hist_spec = pl.BlockSpec((s_blk, A_HEAD, 2 * D_FF), lambda b, t: (b, 0, 0))
    prev_spec = pl.BlockSpec((None, s_blk, A_HEAD, 2 * D_FF), lambda b, t: (layer, b, 0, 0))
    consts = [g2, wu, cw, cb, wd]
    return pl.pallas_call(
        functools.partial(_ffn_body, s_blk=s_blk, t_blk=t_blk, final=final),
        grid=grid,
        in_specs=[tile_spec, prev_spec] + [_layer_spec(layer, c.shape) for c in consts] + [_const_spec(final_g.shape)],
        out_specs=[tile_spec, hist_spec],
        out_shape=[jax.ShapeDtypeStruct((nb, seq, D_MODEL), F32),
                   jax.ShapeDtypeStruct((nb, A_HEAD, 2 * D_FF), F32)],
        scratch_shapes=([pltpu.VMEM((s_blk, A_HEAD, 2 * D_FF), F32)]
                        + [pltpu.VMEM((sub_rows, D_MODEL), BF16)] * n_slots
                        + [pltpu.VMEM((sub_rows, D_FF), BF16)] * n_slots),
        compiler_params=pltpu.CompilerParams(dimension_semantics=("arbitrary", "arbitrary"),
                                             vmem_limit_bytes=VMEM_LIMIT),
        name="ffn_prompt" if prompt else "ffn_sample",
    )(x, prev_f, *consts, final_g)


def _rel_bias_toeplitz(rel_bias_l, n_q, n_k):
    lo = ATTN_WINDOW - (n_k - 1) + REL_CLIP
    hi = ATTN_WINDOW + (n_q - 1) + REL_CLIP
    pad_lo, pad_hi = max(0, -lo), max(0, hi - 2 * REL_CLIP)
    ext = jnp.pad(rel_bias_l, ((0, 0), (pad_lo, pad_hi)), mode="edge")[:, lo + pad_lo:hi + pad_lo + 1]
    period = n_q + n_k
    w = jnp.pad(ext[:, ::-1], ((0, 0), (0, 1)))
    flat = jnp.tile(w, (1, n_q))[:, :n_q * (period - 1)]
    return flat.reshape(-1, n_q, period - 1)[:, :, n_q - 1:n_q - 1 + n_k].astype(F32)


def _prompt_bias(rel_bias):
    r = np.arange(Q_PAIR)[:, None]
    kk = np.arange(K_PAIR)[None, :] - CHUNK * (r // CHUNK)
    in_band = (kk >= 0) & (kk < ATTN_WINDOW + CHUNK)
    depth = rel_bias.shape[0]
    tab = _rel_bias_toeplitz(rel_bias.reshape(depth * N_HEADS, -1), Q_PAIR, K_PAIR)
    return jnp.where(in_band[None], tab * LOG2E, NEG_INF).reshape(depth, N_HEADS * Q_PAIR, K_PAIR)


def _sample_bias(rel_bias, t):
    depth = rel_bias.shape[0]
    tab = _rel_bias_toeplitz(rel_bias.reshape(depth * N_HEADS, -1), t, ATTN_WINDOW + t)
    tab = tab.reshape(depth, N_HEADS * t, ATTN_WINDOW + t) * LOG2E
    return tab[:, :, :ATTN_WINDOW], tab[:, :, ATTN_WINDOW:]


def _pad_history(prev, head):
    return jnp.pad(prev, ((0, 0), (0, 0), (head - prev.shape[2], 0), (0, 0)))


def kernel(x_prompt, x_sample, cache_conv_a, cache_conv_b, cache_ffn_conv, cache_attn_k, cache_attn_v, cache_mem_k,
           cache_mem_v, mem_prompt, norm1_g, w_in, w_mem_kv, conv_a_w, conv_b_w, conv_b_bias, ln_b_g, ln_b_b, rel_bias,
           grp_norm_g, w_out, norm2_g, w_up, ffn_conv_w, ffn_conv_b, w_down, final_g):
    depth = w_in.shape[0]
    bp, tp, _ = x_prompt.shape
    bs, ts, _ = x_sample.shape
    assert tp % PROMPT_TILE == 0 and tp % FFN_TILE == 0 and SUB_TILE % Q_PAIR == 0 and tp >= ATTN_WINDOW
    assert bs % SAMPLE_STREAMS == 0 and ts % SUBLANES == 0 and ts >= B_HEAD
    assert cache_attn_k.shape[2] == ATTN_WINDOW

    row = lambda p: p.reshape(depth, 1, -1)
    mixer_w = (row(norm1_g), w_in.astype(BF16), conv_a_w, conv_b_w, row(conv_b_bias), row(ln_b_g), row(ln_b_b),
               row(grp_norm_g), w_out.astype(BF16))
    ffn_w = (row(norm2_g), w_up.astype(BF16), ffn_conv_w, row(ffn_conv_b), w_down.astype(BF16))
    final_row = final_g.reshape(1, D_MODEL)
    mem_k_all, mem_v_all = _memkv(mem_prompt, w_mem_kv.astype(BF16))

    prompt_bias = (_prompt_bias(rel_bias),)
    sample_bias = _sample_bias(rel_bias, ts)
    zero_a = jnp.zeros((depth, bp, A_HEAD, G), F32)
    zero_b = jnp.zeros((depth, bp, B_HEAD, G), F32)
    zero_f = jnp.zeros((depth, bp, A_HEAD, 2 * D_FF), F32)
    hist_a = _pad_history(cache_conv_a, A_HEAD)
    hist_b = _pad_history(cache_conv_b, B_HEAD)
    hist_f = _pad_history(cache_ffn_conv, A_HEAD)
    mem_k_s = cache_mem_k.reshape(depth, bs, N_MEM, G)
    mem_v_s = cache_mem_v.reshape(depth, bs, N_MEM, G)
    attn_k_s = cache_attn_k.reshape(depth, bs, ATTN_WINDOW, G)
    attn_v_s = cache_attn_v.reshape(depth, bs, ATTN_WINDOW, G)

    xp, xs = x_prompt, x_sample
    outs = {name: [] for name in ("pa", "pb", "pf", "pk", "pv", "sa", "sb", "sf", "sk", "sv")}
    for l in range(depth):
        final = l == depth - 1
        xp, na, nb, k_p, v_p = _mixer(l, xp, mem_k_all, mem_v_all, zero_a, zero_b, None, None, mixer_w,
                                      prompt_bias, prompt=True)
        xp, nf = _ffn(l, xp, zero_f, ffn_w, final_row, prompt=True, final=final)
        outs["pa"].append(na[:, A_HEAD - (CONV_A_WIDTH - 1):])
        outs["pb"].append(nb[:, B_HEAD - (CONV_B_WIDTH - 1):])
        outs["pf"].append(nf[:, A_HEAD - (FFN_CONV_WIDTH - 1):])
        outs["pk"].append(k_p.reshape(bp, ATTN_WINDOW, N_HEADS, HEAD_DIM))
        outs["pv"].append(v_p.reshape(bp, ATTN_WINDOW, N_HEADS, HEAD_DIM))

        xs, na, nb, k_s, v_s = _mixer(l, xs, mem_k_s, mem_v_s, hist_a, hist_b, attn_k_s, attn_v_s, mixer_w,
                                      sample_bias, prompt=False)
        xs, nf = _ffn(l, xs, hist_f, ffn_w, final_row, prompt=False, final=final)
        outs["sa"].append(na[:, A_HEAD - (CONV_A_WIDTH - 1):])
        outs["sb"].append(nb[:, B_HEAD - (CONV_B_WIDTH - 1):])
        outs["sf"].append(nf[:, A_HEAD - (FFN_CONV_WIDTH - 1):])
        outs["sk"].append(k_s.reshape(bs, ts, N_HEADS, HEAD_DIM))
        outs["sv"].append(v_s.reshape(bs, ts, N_HEADS, HEAD_DIM))

    st = {name: jnp.stack(v) for name, v in outs.items()}
    mem_shape = (depth, bp, N_MEM, N_HEADS, HEAD_DIM)
    return (xp, xs, st["pa"], st["pb"], st["pf"], st["pk"], st["pv"],
            mem_k_all.reshape(mem_shape), mem_v_all.reshape(mem_shape),
            st["sa"], st["sb"], st["sf"], st["sk"], st["sv"])
```

```python
import functools

import numpy as np
import jax
import jax.numpy as jnp
from jax import lax
from jax.experimental import pallas as pl
from jax.experimental.pallas import tpu as pltpu

F32 = jnp.float32
BF16 = jnp.bfloat16

D_MODEL = 1024
G = 256
N_HEADS = 4
HEAD_DIM = 64
CHUNK = 64
N_MEM = 256
N_PREV_CHUNKS = 8
ATTN_WINDOW = N_PREV_CHUNKS * CHUNK
REL_CLIP = 128
CONV_A_WIDTH = 3
CONV_B_WIDTH = 31
FFN_CONV_WIDTH = 3
D_FF = 2816
D_IN_PROJ = 9 * G
RMS_EPS = 1e-6
LN_EPS = 1e-5
NEG_INF = -1e30
QK_SCALE = HEAD_DIM ** -0.5
LOG2E = 1.4426950408889634
Q_SCALE = QK_SCALE * LOG2E

SUBLANES = 8
A_HEAD = SUBLANES
B_HEAD = 32
Q_PAIR = 2 * CHUNK
K_PAIR = ATTN_WINDOW + Q_PAIR
FF_CHUNK = 256
N_FF_CHUNKS = D_FF // FF_CHUNK
VMEM_LIMIT = 56 * 1024 * 1024

PROMPT_TILE = 1024
FFN_TILE = 512
SUB_TILE = 256
PROJ_COLS = 256
CONV_B_ROWS = 32
OUT_EVERY = 2
DOWN_EVERY = 3
SAMPLE_STREAMS = 8
MEM_STREAMS = 4


def _rms(x, g):
    return x * lax.rsqrt(jnp.mean(x * x, axis=-1, keepdims=True) + RMS_EPS) * g


def _dot(a, b):
    return jnp.dot(a, b, preferred_element_type=F32)


def _dot_nt(a, b):
    return lax.dot_general(a, b, (((1,), (1,)), ((), ())), preferred_element_type=F32)


def _silu(x):
    return x * jax.nn.sigmoid(x)


def _interleave(stages, every=None):
    every = every or (1,) * len(stages)
    live = list(range(len(stages)))
    rounds = 0
    while live:
        for k in list(live):
            if len(live) == 1 or rounds % every[k] == every[k] - 1:
                try:
                    next(stages[k])
                except StopIteration:
                    live.remove(k)
        rounds += 1


def _roll_rows(x, shift):
    shift = shift % x.shape[0]
    return x if shift == 0 else pltpu.roll(x, shift, axis=0)


def _shift_rows(u, hist, s):
    moved = _roll_rows(u, s)
    sub = lax.broadcasted_iota(jnp.int32, (SUBLANES, u.shape[1]), 0)
    top = jnp.where(sub < s, _roll_rows(hist, s), moved[0:SUBLANES])
    return jnp.concatenate([top, moved[SUBLANES:]], axis=0)


def _conv3(u, hist, w):
    return w[2:3] * u + w[1:2] * _shift_rows(u, hist, 1) + w[0:1] * _shift_rows(u, hist, 2)


def _memkv_body(mem_ref, w_ref, mk_ref, mv_ref):
    nb = mem_ref.shape[0]
    kv = _dot(mem_ref[...].reshape(nb * N_MEM, D_MODEL).astype(BF16), w_ref[0]).reshape(nb, N_MEM, 2 * G)
    mk_ref[0] = kv[:, :, :G]
    mv_ref[0] = kv[:, :, G:]


def _memkv(mem_prompt, w_mem_kv_bf):
    depth = w_mem_kv_bf.shape[0]
    b = mem_prompt.shape[0]
    nb = MEM_STREAMS if b % MEM_STREAMS == 0 else 1
    out = jax.ShapeDtypeStruct((depth, b, N_MEM, G), F32)
    return pl.pallas_call(
        _memkv_body,
        grid=(b // nb, depth),
        in_specs=[pl.BlockSpec((nb, N_MEM, D_MODEL), lambda i, l: (i, 0, 0)),
                  pl.BlockSpec((1, D_MODEL, 2 * G), lambda i, l: (l, 0, 0))],
        out_specs=[pl.BlockSpec((1, nb, N_MEM, G), lambda i, l: (l, i, 0, 0)),
                   pl.BlockSpec((1, nb, N_MEM, G), lambda i, l: (l, i, 0, 0))],
        out_shape=[out, out],
        compiler_params=pltpu.CompilerParams(dimension_semantics=("arbitrary", "arbitrary"),
                                             vmem_limit_bytes=VMEM_LIMIT),
        name="memkv",
    )(mem_prompt, w_mem_kv_bf)


def _softmax_parts(s_list):
    m = s_list[0].max(axis=-1, keepdims=True)
    for s in s_list[1:]:
        m = jnp.maximum(m, s.max(axis=-1, keepdims=True))
    e_list = [jnp.exp2(s - m) for s in s_list]
    l = e_list[0].sum(axis=-1, keepdims=True)
    for e in e_list[1:]:
        l = l + e.sum(axis=-1, keepdims=True)
    return e_list, 1.0 / l


def _stack_heads(q):
    head = lax.broadcasted_iota(jnp.int32, q.shape, 1) // HEAD_DIM
    return jnp.concatenate([jnp.where(head == hd, q, 0.0) for hd in range(N_HEADS)], axis=0).astype(BF16)


def _unstack_heads(o):
    t = o.shape[0] // N_HEADS
    head = lax.broadcasted_iota(jnp.int32, (t, G), 1) // HEAD_DIM
    out = o[0:t]
    for hd in range(1, N_HEADS):
        out = jnp.where(head == hd, o[hd * t:(hd + 1) * t], out)
    return out


def _mixer_body(*refs, prompt, s_blk, t_blk, n_slots):
    if prompt:
        (x_ref, mk_ref, mv_ref, prev_a_ref, prev_b_ref, g1_ref, w_in_ref, caw_ref, cbw_ref, cbb_ref,
         lng_ref, lnb_ref, bias_ref, gng_ref, w_out_ref,
         x1_ref, new_a_ref, new_b_ref, k_out_ref, v_out_ref,
         hist_a, hist_b, *slots, kbuf, vbuf) = refs
    else:
        (x_ref, mk_ref, mv_ref, prev_a_ref, prev_b_ref, ck_ref, cv_ref, g1_ref, w_in_ref, caw_ref, cbw_ref,
         cbb_ref, lng_ref, lnb_ref, bias_ref, bias2_ref, gng_ref, w_out_ref,
         x1_ref, new_a_ref, new_b_ref, k_out_ref, v_out_ref,
         hist_a, hist_b, *slots) = refs
    h_scr, z_scr, y_scr = slots[:n_slots], slots[n_slots:2 * n_slots], slots[2 * n_slots:]
    sub = min(t_blk, SUB_TILE)
    n_sub = t_blk // sub
    rows = s_blk * sub
    t = pl.program_id(1)

    def group_norm(y, gi):
        return _rms(y, gng_ref[:, gi * G:(gi + 1) * G]).astype(BF16)

    @pl.when(t == 0)
    def _():
        hist_a[...] = prev_a_ref[...]
        hist_b[...] = prev_b_ref[...]

    ha = [hist_a[s] for s in range(s_blk)]
    hb = [hist_b[s] for s in range(s_blk)]

    def stage_in(i):
        slot, r0 = i % n_slots, i * sub
        x = x_ref[:, r0:r0 + sub, :].reshape(rows, D_MODEL)
        h_scr[slot][...] = _rms(x, g1_ref[...]).astype(BF16)
        yield
        for c in range(0, D_IN_PROJ, PROJ_COLS):
            z_scr[slot][:, c:c + PROJ_COLS] = _dot(h_scr[slot][...],w_in_ref[:, c:c + PROJ_COLS])
            yield

    def stage_conv(i):
        slot = i % n_slots
        caw = caw_ref[...]
        for s in range(s_blk):
            sl = slice(s * sub, (s + 1) * sub)
            p = z_scr[slot][sl, G:2 * G] * z_scr[slot][sl, 2 * G:3 * G]
            y_scr[slot][sl, 0:G] = group_norm(z_scr[slot][sl, 0:G] * _conv3(p, ha[s], caw), 0)
            ha[s] = p[sub - A_HEAD:]
        yield
        cbw = cbw_ref[...]
        first = B_HEAD - (CONV_B_WIDTH - 1)
        r_step = min(sub, CONV_B_ROWS)
        for s in range(s_blk):
            sl = slice(s * sub, (s + 1) * sub)
            glu = z_scr[slot][sl, 3 * G:4 * G] * jax.nn.sigmoid(z_scr[slot][sl, 4 * G:5 * G])
            xs = jnp.concatenate([hb[s], glu], axis=0)
            hb[s] = glu[sub - B_HEAD:]
            moved = [_roll_rows(xs, -b) for b in range(SUBLANES)]
            if s_blk == 1:
                yield
            for q0 in range(0, sub, r_step):
                acc = None
                for j in range(CONV_B_WIDTH):
                    off = first + j
                    lo = q0 + (off // SUBLANES) * SUBLANES
                    term = cbw[j:j + 1] * moved[off % SUBLANES][lo:lo + r_step]
                    acc = term if acc is None else acc + term
                c = acc + cbb_ref[...]
                cc = c - jnp.mean(c, axis=-1, keepdims=True)
                c = cc * lax.rsqrt(jnp.mean(cc * cc, axis=-1, keepdims=True) + LN_EPS) * lng_ref[...] + lnb_ref[...]
                y_scr[slot][s * sub + q0:s * sub + q0 + r_step, G:2 * G] = group_norm(_silu(c), 1)
                if s_blk == 1:
                    yield
            if s_blk > 1 and s % 2 == 1:
                yield

    def stage_attn(i):
        slot, r0 = i % n_slots, i * sub
        k_new = z_scr[slot][:, 6 * G:7 * G]
        v_new = z_scr[slot][:, 7 * G:8 * G]
        kv_off = t_blk - k_out_ref.shape[1]
        if r0 >= kv_off:
            k_out_ref[:, r0 - kv_off:r0 - kv_off + sub, :] = k_new.reshape(s_blk, sub, G)
            v_out_ref[:, r0 - kv_off:r0 - kv_off + sub, :] = v_new.reshape(s_blk, sub, G)
        if not prompt:
            for j in range(s_blk):
                sl = slice(j * sub, (j + 1) * sub)
                qs = _stack_heads(z_scr[slot][sl, 5 * G:6 * G] * Q_SCALE)
                kn = z_scr[slot][sl, 6 * G:7 * G].astype(BF16)
                vn = z_scr[slot][sl, 7 * G:8 * G].astype(BF16)
                s_old = _dot_nt(qs, ck_ref[j].astype(BF16)) + bias_ref[...]
                s_new = _dot_nt(qs, kn) + bias2_ref[...]
                (e_old, e_new), inv = _softmax_parts([s_old, s_new])
                o = (_dot(e_old.astype(BF16), cv_ref[j].astype(BF16)) + _dot(e_new.astype(BF16), vn)) * inv
                y_scr[slot][sl, 2 * G:3 * G] = group_norm(_unstack_heads(o), 2)
                qs = _stack_heads(z_scr[slot][sl, 8 * G:9 * G] * Q_SCALE)
                (e,), inv = _softmax_parts([_dot_nt(qs, mk_ref[j].astype(BF16))])
                o = _dot(e.astype(BF16), mv_ref[j].astype(BF16)) * inv
                y_scr[slot][sl, 3 * G:4 * G] = group_norm(_unstack_heads(o), 3)
                yield
            return

        if i == 0:
            @pl.when(t == 0)
            def _():
                kbuf[0:ATTN_WINDOW, :] = jnp.zeros((ATTN_WINDOW, G), BF16)
                vbuf[0:ATTN_WINDOW, :] = jnp.zeros((ATTN_WINDOW, G), BF16)

        row0 = pl.multiple_of(t * t_blk + r0, sub)
        kbuf[pl.ds(ATTN_WINDOW + row0, sub), :] = k_new.astype(BF16)
        vbuf[pl.ds(ATTN_WINDOW + row0, sub), :] = v_new.astype(BF16)
        yield
        for lo in range(0, sub, Q_PAIR):
            base = pl.multiple_of(row0 + lo, Q_PAIR)
            qs = _stack_heads(z_scr[slot][lo:lo + Q_PAIR, 5 * G:6 * G] * Q_SCALE)
            sc = _dot_nt(qs, kbuf[pl.ds(base, K_PAIR), :]) + bias_ref[...]
            if r0 + lo < ATTN_WINDOW:
                key_ok = (lax.broadcasted_iota(jnp.int32, (1, K_PAIR), 1) + base) >= ATTN_WINDOW
                sc = jnp.where(key_ok, sc, NEG_INF)
            (e,), inv = _softmax_parts([sc])
            yield
            o = _dot(e.astype(BF16), vbuf[pl.ds(base, K_PAIR), :]) * inv
            y_scr[slot][lo:lo + Q_PAIR, 2 * G:3 * G] = group_norm(_unstack_heads(o), 2)
            yield
            qs = _stack_heads(z_scr[slot][lo:lo + Q_PAIR, 8 * G:9 * G] * Q_SCALE)
            (e,), inv = _softmax_parts([_dot_nt(qs, mk_ref[0].astype(BF16))])
            yield
            o = _dot(e.astype(BF16), mv_ref[0].astype(BF16)) * inv
            y_scr[slot][lo:lo + Q_PAIR, 3 * G:4 * G] = group_norm(_unstack_heads(o), 3)
            yield

    def stage_out(i):
        slot, r0 = i % n_slots, i * sub
        for c in range(0, D_MODEL, PROJ_COLS):
            x = x_ref[:, r0:r0 + sub, c:c + PROJ_COLS].reshape(rows, PROJ_COLS)
            x1 = x + _dot(y_scr[slot][...],w_out_ref[:, c:c + PROJ_COLS])
            x1_ref[:, r0:r0 + sub, c:c + PROJ_COLS] = x1.reshape(s_blk, sub, PROJ_COLS)
            yield

    _interleave([stage_in(0)])
    for i in range(n_sub):
        stages, every = [stage_conv(i), stage_attn(i)], [1, 1]
        if i + 1 < n_sub:
            stages.append(stage_in(i + 1))
            every.append(1)
        if i > 0:
            stages.append(stage_out(i - 1))
            every.append(OUT_EVERY)
        _interleave(stages, every)
    _interleave([stage_out(n_sub - 1)])
    for s in range(s_blk):
        new_a_ref[s] = ha[s]
        new_b_ref[s] = hb[s]
        hist_a[s] = ha[s]
        hist_b[s] = hb[s]


def _const_spec(shape):
    zeros = (0,) * len(shape)
    return pl.BlockSpec(shape, lambda b, t: zeros)


def _layer_spec(layer, shape):
    zeros = (0,) * (len(shape) - 1)
    return pl.BlockSpec((None,) + tuple(shape[1:]), lambda b, t: (layer,) + zeros, pipeline_mode=pl.Buffered(1))


def _mixer(layer, x, mk, mv, prev_a, prev_b, cache_k, cache_v, weights, bias_tables, *, prompt):
    (g1, w_in, caw, cbw, cbb, lng, lnb, gng, w_out) = weights
    nb, seq, _ = x.shape
    if prompt:
        s_blk, t_blk = 1, PROMPT_TILE
    else:
        s_blk, t_blk = SAMPLE_STREAMS, seq
    grid = (nb // s_blk, seq // t_blk)
    nt = grid[1]
    rows = s_blk * t_blk

    stream_spec = lambda r, c: pl.BlockSpec((s_blk, r, c), lambda b, t: (b, 0, 0))
    layer_stream_spec = lambda r, c: pl.BlockSpec((None, s_blk, r, c), lambda b, t: (layer, b, 0, 0))
    tile_spec = lambda c: pl.BlockSpec((s_blk, t_blk, c), lambda b, t: (b, t, 0))
    in_specs = [tile_spec(D_MODEL), layer_stream_spec(N_MEM, G), layer_stream_spec(N_MEM, G),
                layer_stream_spec(A_HEAD, G), layer_stream_spec(B_HEAD, G)]
    args = [x, mk, mv, prev_a, prev_b]
    if not prompt:
        in_specs += [layer_stream_spec(ATTN_WINDOW, G), layer_stream_spec(ATTN_WINDOW, G)]
        args += [cache_k, cache_v]
    consts = [g1, w_in, caw, cbw, cbb, lng, lnb, *bias_tables, gng, w_out]
    in_specs += [_layer_spec(layer, c.shape) for c in consts]
    args += consts

    if prompt:
        kv_blk = min(t_blk, ATTN_WINDOW)
        first_kept = nt - ATTN_WINDOW // kv_blk
        kv_spec = pl.BlockSpec((1, kv_blk, G), lambda b, t: (b, jnp.maximum(t - first_kept, 0), 0))
        kv_rows = ATTN_WINDOW
    else:
        kv_spec = tile_spec(G)
        kv_rows = seq
    out_specs = [tile_spec(D_MODEL), stream_spec(A_HEAD, G), stream_spec(B_HEAD, G), kv_spec, kv_spec]
    out_shape = [jax.ShapeDtypeStruct((nb, seq, D_MODEL), F32),
                 jax.ShapeDtypeStruct((nb, A_HEAD, G), F32),
                 jax.ShapeDtypeStruct((nb, B_HEAD, G), F32),
                 jax.ShapeDtypeStruct((nb, kv_rows, G), F32),
                 jax.ShapeDtypeStruct((nb, kv_rows, G), F32)]
    sub_rows = s_blk * min(t_blk, SUB_TILE)
    n_slots = min(2, rows // sub_rows)
    scratch = ([pltpu.VMEM((s_blk, A_HEAD, G), F32), pltpu.VMEM((s_blk, B_HEAD, G), F32)]
               + [pltpu.VMEM((sub_rows, D_MODEL), BF16)] * n_slots
               + [pltpu.VMEM((sub_rows, D_IN_PROJ), F32)] * n_slots
               + [pltpu.VMEM((sub_rows, 4 * G), BF16)] * n_slots)
    if prompt:
        scratch += [pltpu.VMEM((ATTN_WINDOW + seq, G), BF16), pltpu.VMEM((ATTN_WINDOW + seq, G), BF16)]
    return pl.pallas_call(
        functools.partial(_mixer_body, prompt=prompt, s_blk=s_blk, t_blk=t_blk, n_slots=n_slots),
        grid=grid, in_specs=in_specs, out_specs=out_specs, out_shape=out_shape, scratch_shapes=scratch,
        compiler_params=pltpu.CompilerParams(dimension_semantics=("arbitrary", "arbitrary"),
                                             vmem_limit_bytes=VMEM_LIMIT),
        name="mixer_prompt" if prompt else "mixer_sample",
    )(*args)


def _ffn_body(x_ref, prev_ref, g2_ref, wu_ref, cw_ref, cb_ref, wd_ref, gf_ref,
              out_ref, new_f_ref, carry, *slots, s_blk, t_blk, final):
    sub = min(t_blk, SUB_TILE)
    n_sub = t_blk // sub
    n_slots = len(slots) // 2
    h_scr, act_scr = slots[:n_slots], slots[n_slots:]
    rows = s_blk * sub
    t = pl.program_id(1)

    @pl.when(t == 0)
    def _():
        carry[...] = prev_ref[...]

    cols = [c for j in range(N_FF_CHUNKS) for c in (j * FF_CHUNK, D_FF + j * FF_CHUNK)]
    hist = {c: [carry[s, :, c:c + FF_CHUNK] for s in range(s_blk)] for c in cols}

    def stage_gated(i):
        slot, r0 = i % n_slots, i * sub
        x = x_ref[:, r0:r0 + sub, :].reshape(rows, D_MODEL)
        h_scr[slot][...] = _rms(x, g2_ref[...]).astype(BF16)
        yield
        for j in range(N_FF_CHUNKS):
            halves = []
            for col in (j * FF_CHUNK, D_FF + j * FF_CHUNK):
                u = _dot(h_scr[slot][...],wu_ref[:, col:col + FF_CHUNK])
                cw = cw_ref[:, col:col + FF_CHUNK]
                cb = cb_ref[:, col:col + FF_CHUNK]
                per_stream = []
                for s in range(s_blk):
                    us = u[s * sub:(s + 1) * sub]
                    per_stream.append(_conv3(us, hist[col][s], cw) + cb)
                    hist[col][s] = us[sub - A_HEAD:]
                halves.append(per_stream[0] if s_blk == 1 else jnp.concatenate(per_stream, axis=0))
            val, gate = halves
            act_scr[slot][:, j * FF_CHUNK:(j + 1) * FF_CHUNK] = (_silu(gate) * val).astype(BF16)
            yield

    def stage_down(i):
        slot, r0 = i % n_slots, i * sub
        for c in range(0, D_MODEL, PROJ_COLS):
            x = x_ref[:, r0:r0 + sub, c:c + PROJ_COLS].reshape(rows, PROJ_COLS)
            out = x + _dot(act_scr[slot][...],wd_ref[:, c:c + PROJ_COLS])
            out_ref[:, r0:r0 + sub, c:c + PROJ_COLS] = out.reshape(s_blk, sub, PROJ_COLS)
            yield
        if final:
            out = out_ref[:, r0:r0 + sub, :].reshape(rows, D_MODEL)
            out_ref[:, r0:r0 + sub, :] = _rms(out, gf_ref[...]).reshape(s_blk, sub, D_MODEL)
            yield

    _interleave([stage_gated(0)])
    for i in range(1, n_sub):
        _interleave([stage_gated(i), stage_down(i - 1)], every=(1, DOWN_EVERY))
    _interleave([stage_down(n_sub - 1)])

    for c in cols:
        for s in range(s_blk):
            new_f_ref[s, :, c:c + FF_CHUNK] = hist[c][s]
            carry[s, :, c:c + FF_CHUNK] = hist[c][s]


def _ffn(layer, x, prev_f, weights, final_g, *, prompt, final):
    (g2, wu, cw, cb, wd) = weights
    nb, seq, _ = x.shape
    if prompt:
        s_blk, t_blk = 1, FFN_TILE
    else:
        s_blk, t_blk = SAMPLE_STREAMS, seq
    grid = (nb // s_blk, seq // t_blk)
    sub_rows = s_blk * min(t_blk, SUB_TILE)
    n_slots = min(2, s_blk * t_blk // sub_rows)
    tile_spec = pl.BlockSpec((s_blk, t_blk, D_MODEL), lambda b, t: (b, t, 0))
    hist_spec = pl.BlockSpec((s_blk, A_HEAD, 2 * D_FF), lambda b, t: (b, 0, 0))
    prev_spec = pl.BlockSpec((None, s_blk, A_HEAD, 2 * D_FF), lambda b, t: (layer, b, 0, 0))
    consts = [g2, wu, cw, cb, wd]
    return pl.pallas_call(
        functools.partial(_ffn_body, s_blk=s_blk, t_blk=t_blk, final=final),
        grid=grid,
        in_specs=[tile_spec, prev_spec] + [_layer_spec(layer, c.shape) for c in consts] + [_const_spec(final_g.shape)],
        out_specs=[tile_spec, hist_spec],
        out_shape=[jax.ShapeDtypeStruct((nb, seq, D_MODEL), F32),
                   jax.ShapeDtypeStruct((nb, A_HEAD, 2 * D_FF), F32)],
        scratch_shapes=([pltpu.VMEM((s_blk, A_HEAD, 2 * D_FF), F32)]
                        + [pltpu.VMEM((sub_rows, D_MODEL), BF16)] * n_slots
                        + [pltpu.VMEM((sub_rows, D_FF), BF16)] * n_slots),
        compiler_params=pltpu.CompilerParams(dimension_semantics=("arbitrary", "arbitrary"),
                                             vmem_limit_bytes=VMEM_LIMIT),
        name="ffn_prompt" if prompt else "ffn_sample",
    )(x, prev_f, *consts, final_g)


def _rel_bias_toeplitz(rel_bias_l, n_q, n_k):
    lo = ATTN_WINDOW - (n_k - 1) + REL_CLIP
    hi = ATTN_WINDOW + (n_q - 1) + REL_CLIP
    pad_lo, pad_hi = max(0, -lo), max(0, hi - 2 * REL_CLIP)
    ext = jnp.pad(rel_bias_l, ((0, 0), (pad_lo, pad_hi)), mode="edge")[:, lo + pad_lo:hi + pad_lo + 1]
    period = n_q + n_k
    w = jnp.pad(ext[:, ::-1], ((0, 0), (0, 1)))
    flat = jnp.tile(w, (1, n_q))[:, :n_q * (period - 1)]
    return flat.reshape(-1, n_q, period - 1)[:, :, n_q - 1:n_q - 1 + n_k].astype(F32)


def _prompt_bias(rel_bias):
    r = np.arange(Q_PAIR)[:, None]
    kk = np.arange(K_PAIR)[None, :] - CHUNK * (r // CHUNK)
    in_band = (kk >= 0) & (kk < ATTN_WINDOW + CHUNK)
    depth = rel_bias.shape[0]
    tab = _rel_bias_toeplitz(rel_bias.reshape(depth * N_HEADS, -1), Q_PAIR, K_PAIR)
    return jnp.where(in_band[None], tab * LOG2E, NEG_INF).reshape(depth, N_HEADS * Q_PAIR, K_PAIR)


def _sample_bias(rel_bias, t):
    depth = rel_bias.shape[0]
    tab = _rel_bias_toeplitz(rel_bias.reshape(depth * N_HEADS, -1), t, ATTN_WINDOW + t)
    tab = tab.reshape(depth, N_HEADS * t, ATTN_WINDOW + t) * LOG2E
    return tab[:, :, :ATTN_WINDOW], tab[:, :, ATTN_WINDOW:]


def _pad_history(prev, head):
    return jnp.pad(prev, ((0, 0), (0, 0), (head - prev.shape[2], 0), (0, 0)))


def kernel(x_prompt, x_sample, cache_conv_a, cache_conv_b, cache_ffn_conv, cache_attn_k, cache_attn_v, cache_mem_k,
           cache_mem_v, mem_prompt, norm1_g, w_in, w_mem_kv, conv_a_w, conv_b_w, conv_b_bias, ln_b_g, ln_b_b, rel_bias,
           grp_norm_g, w_out, norm2_g, w_up, ffn_conv_w, ffn_conv_b, w_down, final_g):
    depth = w_in.shape[0]
    bp, tp, _ = x_prompt.shape
    bs, ts, _ = x_sample.shape
    assert tp % PROMPT_TILE == 0 and tp % FFN_TILE == 0 and SUB_TILE % Q_PAIR == 0 and tp >= ATTN_WINDOW
    assert bs % SAMPLE_STREAMS == 0 and ts % SUBLANES == 0 and ts >= B_HEAD
    assert cache_attn_k.shape[2] == ATTN_WINDOW

    row = lambda p: p.reshape(depth, 1, -1)
    mixer_w = (row(norm1_g), w_in.astype(BF16), conv_a_w, conv_b_w, row(conv_b_bias), row(ln_b_g), row(ln_b_b),
               row(grp_norm_g), w_out.astype(BF16))
    ffn_w = (row(norm2_g), w_up.astype(BF16), ffn_conv_w, row(ffn_conv_b), w_down.astype(BF16))
    final_row = final_g.reshape(1, D_MODEL)
    mem_k_all, mem_v_all = _memkv(mem_prompt, w_mem_kv.astype(BF16))

    prompt_bias = (_prompt_bias(rel_bias),)
    sample_bias = _sample_bias(rel_bias, ts)
    zero_a = jnp.zeros((depth, bp, A_HEAD, G), F32)
    zero_b = jnp.zeros((depth, bp, B_HEAD, G), F32)
    zero_f = jnp.zeros((depth, bp, A_HEAD, 2 * D_FF), F32)
    hist_a = _pad_history(cache_conv_a, A_HEAD)
    hist_b = _pad_history(cache_conv_b, B_HEAD)
    hist_f = _pad_history(cache_ffn_conv, A_HEAD)
    mem_k_s = cache_mem_k.reshape(depth, bs, N_MEM, G)
    mem_v_s = cache_mem_v.reshape(depth, bs, N_MEM, G)
    attn_k_s = cache_attn_k.reshape(depth, bs, ATTN_WINDOW, G)
    attn_v_s = cache_attn_v.reshape(depth, bs, ATTN_WINDOW, G)

    xp, xs = x_prompt, x_sample
    outs = {name: [] for name in ("pa", "pb", "pf", "pk", "pv", "sa", "sb", "sf", "sk", "sv")}
    for l in range(depth):
        final = l == depth - 1
        xp, na, nb, k_p, v_p = _mixer(l, xp, mem_k_all, mem_v_all, zero_a, zero_b, None, None, mixer_w,
                                      prompt_bias, prompt=True)
        xp, nf = _ffn(l, xp, zero_f, ffn_w, final_row, prompt=True, final=final)
        outs["pa"].append(na[:, A_HEAD - (CONV_A_WIDTH - 1):])
        outs["pb"].append(nb[:, B_HEAD - (CONV_B_WIDTH - 1):])
        outs["pf"].append(nf[:, A_HEAD - (FFN_CONV_WIDTH - 1):])
        outs["pk"].append(k_p.reshape(bp, ATTN_WINDOW, N_HEADS, HEAD_DIM))
        outs["pv"].append(v_p.reshape(bp, ATTN_WINDOW, N_HEADS, HEAD_DIM))

        xs, na, nb, k_s, v_s = _mixer(l, xs, mem_k_s, mem_v_s, hist_a, hist_b, attn_k_s, attn_v_s, mixer_w,
                                      sample_bias, prompt=False)
        xs, nf = _ffn(l, xs, hist_f, ffn_w, final_row, prompt=False, final=final)
        outs["sa"].append(na[:, A_HEAD - (CONV_A_WIDTH - 1):])
        outs["sb"].append(nb[:, B_HEAD - (CONV_B_WIDTH - 1):])
        outs["sf"].append(nf[:, A_HEAD - (FFN_CONV_WIDTH - 1):])
        outs["sk"].append(k_s.reshape(bs, ts, N_HEADS, HEAD_DIM))
        outs["sv"].append(v_s.reshape(bs, ts, N_HEADS, HEAD_DIM))

    st = {name: jnp.stack(v) for name, v in outs.items()}
    mem_shape = (depth, bp, N_MEM, N_HEADS, HEAD_DIM)
    return (xp, xs, st["pa"], st["pb"], st["pf"], st["pk"], st["pv"],
            mem_k_all.reshape(mem_shape), mem_v_all.reshape(mem_shape),
            st["sa"], st["sb"], st["sf"], st["sk"], st["sv"])
```

```python
import functools

import numpy as np
import jax
import jax.numpy as jnp
from jax import lax
from jax.experimental import pallas as pl
from jax.experimental.pallas import tpu as pltpu

F32 = jnp.float32
BF16 = jnp.bfloat16

D_MODEL = 1024
G = 256
N_HEADS = 4
HEAD_DIM = 64
CHUNK = 64
N_MEM = 256
N_PREV_CHUNKS = 8
ATTN_WINDOW = N_PREV_CHUNKS * CHUNK
REL_CLIP = 128
CONV_A_WIDTH = 3
CONV_B_WIDTH = 31
FFN_CONV_WIDTH = 3
D_FF = 2816
D_IN_PROJ = 9 * G
RMS_EPS = 1e-6
LN_EPS = 1e-5
NEG_INF = -1e30
QK_SCALE = HEAD_DIM ** -0.5
LOG2E = 1.4426950408889634
Q_SCALE = QK_SCALE * LOG2E

SUBLANES = 8
A_HEAD = SUBLANES
B_HEAD = 32
Q_PAIR = 2 * CHUNK
K_PAIR = ATTN_WINDOW + Q_PAIR
FF_CHUNK = 256
N_FF_CHUNKS = D_FF // FF_CHUNK
VMEM_LIMIT = 56 * 1024 * 1024

PROMPT_TILE = 2048
FFN_TILE = 512
SUB_TILE = 256
PROJ_COLS = 256
CONV_B_ROWS = 32
OUT_EVERY = 2
DOWN_EVERY = 3
SAMPLE_STREAMS = 8
MEM_STREAMS = 4


def _rms(x, g):
    return x * lax.rsqrt(jnp.mean(x * x, axis=-1, keepdims=True) + RMS_EPS) * g


def _dot(a, b):
    return jnp.dot(a, b, preferred_element_type=F32)


def _dot_nt(a, b):
    return lax.dot_general(a, b, (((1,), (1,)), ((), ())), preferred_element_type=F32)


def _silu(x):
    return x * jax.nn.sigmoid(x)


def _interleave(stages, every=None):
    every = every or (1,) * len(stages)
    live = list(range(len(stages)))
    rounds = 0
    while live:
        for k in list(live):
            if len(live) == 1 or rounds % every[k] == every[k] - 1:
                try:
                    next(stages[k])
                except StopIteration:
                    live.remove(k)
        rounds += 1


def _roll_rows(x, shift):
    shift = shift % x.shape[0]
    return x if shift == 0 else pltpu.roll(x, shift, axis=0)


def _shift_rows(u, hist, s):
    moved = _roll_rows(u, s)
    sub = lax.broadcasted_iota(jnp.int32, (SUBLANES, u.shape[1]), 0)
    top = jnp.where(sub < s, _roll_rows(hist, s), moved[0:SUBLANES])
    return jnp.concatenate([top, moved[SUBLANES:]], axis=0)


def _conv3(u, hist, w):
    return w[2:3] * u + w[1:2] * _shift_rows(u, hist, 1) + w[0:1] * _shift_rows(u, hist, 2)


def _memkv_body(mem_ref, w_ref, mk_ref, mv_ref):
    nb = mem_ref.shape[0]
    kv = _dot(mem_ref[...].reshape(nb * N_MEM, D_MODEL).astype(BF16), w_ref[0]).reshape(nb, N_MEM, 2 * G)
    mk_ref[0] = kv[:, :, :G]
    mv_ref[0] = kv[:, :, G:]


def _memkv(mem_prompt, w_mem_kv_bf):
    depth = w_mem_kv_bf.shape[0]
    b = mem_prompt.shape[0]
    nb = MEM_STREAMS if b % MEM_STREAMS == 0 else 1
    out = jax.ShapeDtypeStruct((depth, b, N_MEM, G), F32)
    return pl.pallas_call(
        _memkv_body,
        grid=(b // nb, depth),
        in_specs=[pl.BlockSpec((nb, N_MEM, D_MODEL), lambda i, l: (i, 0, 0)),
                  pl.BlockSpec((1, D_MODEL, 2 * G), lambda i, l: (l, 0, 0))],
        out_specs=[pl.BlockSpec((1, nb, N_MEM, G), lambda i, l: (l, i, 0, 0)),
                   pl.BlockSpec((1, nb, N_MEM, G), lambda i, l: (l, i, 0, 0))],
        out_shape=[out, out],
        compiler_params=pltpu.CompilerParams(dimension_semantics=("arbitrary", "arbitrary"),
                                             vmem_limit_bytes=VMEM_LIMIT),
        name="memkv",
    )(mem_prompt, w_mem_kv_bf)


def _softmax_parts(s_list):
    m = s_list[0].max(axis=-1, keepdims=True)
    for s in s_list[1:]:
        m = jnp.maximum(m, s.max(axis=-1, keepdims=True))
    e_list = [jnp.exp2(s - m) for s in s_list]
    l = e_list[0].sum(axis=-1, keepdims=True)
    for e in e_list[1:]:
        l = l + e.sum(axis=-1, keepdims=True)
    return e_list, 1.0 / l


def _stack_heads(q):
    head = lax.broadcasted_iota(jnp.int32, q.shape, 1) // HEAD_DIM
    return jnp.concatenate([jnp.where(head == hd, q, 0.0) for hd in range(N_HEADS)], axis=0).astype(BF16)


def _unstack_heads(o):
    t = o.shape[0] // N_HEADS
    head = lax.broadcasted_iota(jnp.int32, (t, G), 1) // HEAD_DIM
    out = o[0:t]
    for hd in range(1, N_HEADS):
        out = jnp.where(head == hd, o[hd * t:(hd + 1) * t], out)
    return out


def _mixer_body(*refs, prompt, s_blk, t_blk, n_slots):
    if prompt:
        (x_ref, mk_ref, mv_ref, prev_a_ref, prev_b_ref, g1_ref, w_in_ref, caw_ref, cbw_ref, cbb_ref,
         lng_ref, lnb_ref, bias_ref, gng_ref, w_out_ref,
         x1_ref, new_a_ref, new_b_ref, k_out_ref, v_out_ref,
         hist_a, hist_b, *slots, kbuf, vbuf) = refs
    else:
        (x_ref, mk_ref, mv_ref, prev_a_ref, prev_b_ref, ck_ref, cv_ref, g1_ref, w_in_ref, caw_ref, cbw_ref,
         cbb_ref, lng_ref, lnb_ref, bias_ref, bias2_ref, gng_ref, w_out_ref,
         x1_ref, new_a_ref, new_b_ref, k_out_ref, v_out_ref,
         hist_a, hist_b, *slots) = refs
    h_scr, z_scr, y_scr = slots[:n_slots], slots[n_slots:2 * n_slots], slots[2 * n_slots:]
    sub = min(t_blk, SUB_TILE)
    n_sub = t_blk // sub
    rows = s_blk * sub
    t = pl.program_id(1)

    def group_norm(y, gi):
        return _rms(y, gng_ref[:, gi * G:(gi + 1) * G]).astype(BF16)

    @pl.when(t == 0)
    def _():
        hist_a[...] = prev_a_ref[...]
        hist_b[...] = prev_b_ref[...]

    ha = [hist_a[s] for s in range(s_blk)]
    hb = [hist_b[s] for s in range(s_blk)]

    def stage_in(i):
        slot, r0 = i % n_slots, i * sub
        x = x_ref[:, r0:r0 + sub, :].reshape(rows, D_MODEL)
        h_scr[slot][...] = _rms(x, g1_ref[...]).astype(BF16)
        yield
        for c in range(0, D_IN_PROJ, PROJ_COLS):
            z_scr[slot][:, c:c + PROJ_COLS] = _dot(h_scr[slot][...],w_in_ref[:, c:c + PROJ_COLS])
            yield

    def stage_conv(i):
        slot = i % n_slots
        caw = caw_ref[...]
        for s in range(s_blk):
            sl = slice(s * sub, (s + 1) * sub)
            p = z_scr[slot][sl, G:2 * G] * z_scr[slot][sl, 2 * G:3 * G]
            y_scr[slot][sl, 0:G] = group_norm(z_scr[slot][sl, 0:G] * _conv3(p, ha[s], caw), 0)
            ha[s] = p[sub - A_HEAD:]
        yield
        cbw = cbw_ref[...]
        first = B_HEAD - (CONV_B_WIDTH - 1)
        r_step = min(sub, CONV_B_ROWS)
        for s in range(s_blk):
            sl = slice(s * sub, (s + 1) * sub)
            glu = z_scr[slot][sl, 3 * G:4 * G] * jax.nn.sigmoid(z_scr[slot][sl, 4 * G:5 * G])
            xs = jnp.concatenate([hb[s], glu], axis=0)
            hb[s] = glu[sub - B_HEAD:]
            moved = [_roll_rows(xs, -b) for b in range(SUBLANES)]
            if s_blk == 1:
                yield
            for q0 in range(0, sub, r_step):
                acc = None
                for j in range(CONV_B_WIDTH):
                    off = first + j
                    lo = q0 + (off // SUBLANES) * SUBLANES
                    term = cbw[j:j + 1] * moved[off % SUBLANES][lo:lo + r_step]
                    acc = term if acc is None else acc + term
                c = acc + cbb_ref[...]
                cc = c - jnp.mean(c, axis=-1, keepdims=True)
                c = cc * lax.rsqrt(jnp.mean(cc * cc, axis=-1, keepdims=True) + LN_EPS) * lng_ref[...] + lnb_ref[...]
                y_scr[slot][s * sub + q0:s * sub + q0 + r_step, G:2 * G] = group_norm(_silu(c), 1)
                if s_blk == 1:
                    yield
            if s_blk > 1 and s % 2 == 1:
                yield

    def stage_attn(i):
        slot, r0 = i % n_slots, i * sub
        k_new = z_scr[slot][:, 6 * G:7 * G]
        v_new = z_scr[slot][:, 7 * G:8 * G]
        kv_off = t_blk - k_out_ref.shape[1]
        if r0 >= kv_off:
            k_out_ref[:, r0 - kv_off:r0 - kv_off + sub, :] = k_new.reshape(s_blk, sub, G)
            v_out_ref[:, r0 - kv_off:r0 - kv_off + sub, :] = v_new.reshape(s_blk, sub, G)
        if not prompt:
            for j in range(s_blk):
                sl = slice(j * sub, (j + 1) * sub)
                qs = _stack_heads(z_scr[slot][sl, 5 * G:6 * G] * Q_SCALE)
                kn = z_scr[slot][sl, 6 * G:7 * G].astype(BF16)
                vn = z_scr[slot][sl, 7 * G:8 * G].astype(BF16)
                s_old = _dot_nt(qs, ck_ref[j].astype(BF16)) + bias_ref[...]
                s_new = _dot_nt(qs, kn) + bias2_ref[...]
                (e_old, e_new), inv = _softmax_parts([s_old, s_new])
                o = (_dot(e_old.astype(BF16), cv_ref[j].astype(BF16)) + _dot(e_new.astype(BF16), vn)) * inv
                y_scr[slot][sl, 2 * G:3 * G] = group_norm(_unstack_heads(o), 2)
                qs = _stack_heads(z_scr[slot][sl, 8 * G:9 * G] * Q_SCALE)
                (e,), inv = _softmax_parts([_dot_nt(qs, mk_ref[j].astype(BF16))])
                o = _dot(e.astype(BF16), mv_ref[j].astype(BF16)) * inv
                y_scr[slot][sl, 3 * G:4 * G] = group_norm(_unstack_heads(o), 3)
                yield
            return

        if i == 0:
            @pl.when(t == 0)
            def _():
                kbuf[0:ATTN_WINDOW, :] = jnp.zeros((ATTN_WINDOW, G), BF16)
                vbuf[0:ATTN_WINDOW, :] = jnp.zeros((ATTN_WINDOW, G), BF16)

        row0 = pl.multiple_of(t * t_blk + r0, sub)
        kbuf[pl.ds(ATTN_WINDOW + row0, sub), :] = k_new.astype(BF16)
        vbuf[pl.ds(ATTN_WINDOW + row0, sub), :] = v_new.astype(BF16)
        yield
        for lo in range(0, sub, Q_PAIR):
            base = pl.multiple_of(row0 + lo, Q_PAIR)
            qs = _stack_heads(z_scr[slot][lo:lo + Q_PAIR, 5 * G:6 * G] * Q_SCALE)
            sc = _dot_nt(qs, kbuf[pl.ds(base, K_PAIR), :]) + bias_ref[...]
            if r0 + lo < ATTN_WINDOW:
                key_ok = (lax.broadcasted_iota(jnp.int32, (1, K_PAIR), 1) + base) >= ATTN_WINDOW
                sc = jnp.where(key_ok, sc, NEG_INF)
            (e,), inv = _softmax_parts([sc])
            yield
            o = _dot(e.astype(BF16), vbuf[pl.ds(base, K_PAIR), :]) * inv
            y_scr[slot][lo:lo + Q_PAIR, 2 * G:3 * G] = group_norm(_unstack_heads(o), 2)
            yield
            qs = _stack_heads(z_scr[slot][lo:lo + Q_PAIR, 8 * G:9 * G] * Q_SCALE)
            (e,), inv = _softmax_parts([_dot_nt(qs, mk_ref[0].astype(BF16))])
            yield
            o = _dot(e.astype(BF16), mv_ref[0].astype(BF16)) * inv
            y_scr[slot][lo:lo + Q_PAIR, 3 * G:4 * G] = group_norm(_unstack_heads(o), 3)
            yield

    def stage_out(i):
        slot, r0 = i % n_slots, i * sub
        for c in range(0, D_MODEL, PROJ_COLS):
            x = x_ref[:, r0:r0 + sub, c:c + PROJ_COLS].reshape(rows, PROJ_COLS)
            x1 = x + _dot(y_scr[slot][...],w_out_ref[:, c:c + PROJ_COLS])
            x1_ref[:, r0:r0 + sub, c:c + PROJ_COLS] = x1.reshape(s_blk, sub, PROJ_COLS)
            yield

    _interleave([stage_in(0)])
    for i in range(n_sub):
        stages, every = [stage_conv(i), stage_attn(i)], [1, 1]
        if i + 1 < n_sub:
            stages.append(stage_in(i + 1))
            every.append(1)
        if i > 0:
            stages.append(stage_out(i - 1))
            every.append(OUT_EVERY)
        _interleave(stages, every)
    _interleave([stage_out(n_sub - 1)])
    for s in range(s_blk):
        new_a_ref[s] = ha[s]
        new_b_ref[s] = hb[s]
        hist_a[s] = ha[s]
        hist_b[s] = hb[s]


def _const_spec(shape):
    zeros = (0,) * len(shape)
    return pl.BlockSpec(shape, lambda b, t: zeros)


def _layer_spec(layer, shape):
    zeros = (0,) * (len(shape) - 1)
    return pl.BlockSpec((None,) + tuple(shape[1:]), lambda b, t: (layer,) + zeros, pipeline_mode=pl.Buffered(1))


def _mixer(layer, x, mk, mv, prev_a, prev_b, cache_k, cache_v, weights, bias_tables, *, prompt):
    (g1, w_in, caw, cbw, cbb, lng, lnb, gng, w_out) = weights
    nb, seq, _ = x.shape
    if prompt:
        s_blk, t_blk = 1, PROMPT_TILE
    else:
        s_blk, t_blk = SAMPLE_STREAMS, seq
    grid = (nb // s_blk, seq // t_blk)
    nt = grid[1]
    rows = s_blk * t_blk

    stream_spec = lambda r, c: pl.BlockSpec((s_blk, r, c), lambda b, t: (b, 0, 0))
    layer_stream_spec = lambda r, c: pl.BlockSpec((None, s_blk, r, c), lambda b, t: (layer, b, 0, 0))
    tile_spec = lambda c: pl.BlockSpec((s_blk, t_blk, c), lambda b, t: (b, t, 0))
    in_specs = [tile_spec(D_MODEL), layer_stream_spec(N_MEM, G), layer_stream_spec(N_MEM, G),
                layer_stream_spec(A_HEAD, G), layer_stream_spec(B_HEAD, G)]
    args = [x, mk, mv, prev_a, prev_b]
    if not prompt:
        in_specs += [layer_stream_spec(ATTN_WINDOW, G), layer_stream_spec(ATTN_WINDOW, G)]
        args += [cache_k, cache_v]
    consts = [g1, w_in, caw, cbw, cbb, lng, lnb, *bias_tables, gng, w_out]
    in_specs += [_layer_spec(layer, c.shape) for c in consts]
    args += consts

    if prompt:
        kv_blk = min(t_blk, ATTN_WINDOW)
        first_kept = nt - ATTN_WINDOW // kv_blk
        kv_spec = pl.BlockSpec((1, kv_blk, G), lambda b, t: (b, jnp.maximum(t - first_kept, 0), 0))
        kv_rows = ATTN_WINDOW
    else:
        kv_spec = tile_spec(G)
        kv_rows = seq
    out_specs = [tile_spec(D_MODEL), stream_spec(A_HEAD, G), stream_spec(B_HEAD, G), kv_spec, kv_spec]
    out_shape = [jax.ShapeDtypeStruct((nb, seq, D_MODEL), F32),
                 jax.ShapeDtypeStruct((nb, A_HEAD, G), F32),
                 jax.ShapeDtypeStruct((nb, B_HEAD, G), F32),
                 jax.ShapeDtypeStruct((nb, kv_rows, G), F32),
                 jax.ShapeDtypeStruct((nb, kv_rows, G), F32)]
    sub_rows = s_blk * min(t_blk, SUB_TILE)
    n_slots = min(2, rows // sub_rows)
    scratch = ([pltpu.VMEM((s_blk, A_HEAD, G), F32), pltpu.VMEM((s_blk, B_HEAD, G), F32)]
               + [pltpu.VMEM((sub_rows, D_MODEL), BF16)] * n_slots
               + [pltpu.VMEM((sub_rows, D_IN_PROJ), F32)] * n_slots
               + [pltpu.VMEM((sub_rows, 4 * G), BF16)] * n_slots)
    if prompt:
        scratch += [pltpu.VMEM((ATTN_WINDOW + seq, G), BF16), pltpu.VMEM((ATTN_WINDOW + seq, G), BF16)]
    return pl.pallas_call(
        functools.partial(_mixer_body, prompt=prompt, s_blk=s_blk, t_blk=t_blk, n_slots=n_slots),
        grid=grid, in_specs=in_specs, out_specs=out_specs, out_shape=out_shape, scratch_shapes=scratch,
        compiler_params=pltpu.CompilerParams(dimension_semantics=("arbitrary", "arbitrary"),
                                             vmem_limit_bytes=VMEM_LIMIT),
        name="mixer_prompt" if prompt else "mixer_sample",
    )(*args)


def _ffn_body(x_ref, prev_ref, g2_ref, wu_ref, cw_ref, cb_ref, wd_ref, gf_ref,
              out_ref, new_f_ref, carry, *slots, s_blk, t_blk, final):
    sub = min(t_blk, SUB_TILE)
    n_sub = t_blk // sub
    n_slots = len(slots) // 2
    h_scr, act_scr = slots[:n_slots], slots[n_slots:]
    rows = s_blk * sub
    t = pl.program_id(1)

    @pl.when(t == 0)
    def _():
        carry[...] = prev_ref[...]

    cols = [c for j in range(N_FF_CHUNKS) for c in (j * FF_CHUNK, D_FF + j * FF_CHUNK)]
    hist = {c: [carry[s, :, c:c + FF_CHUNK] for s in range(s_blk)] for c in cols}

    def stage_gated(i):
        slot, r0 = i % n_slots, i * sub
        x = x_ref[:, r0:r0 + sub, :].reshape(rows, D_MODEL)
        h_scr[slot][...] = _rms(x, g2_ref[...]).astype(BF16)
        yield
        for j in range(N_FF_CHUNKS):
            halves = []
            for col in (j * FF_CHUNK, D_FF + j * FF_CHUNK):
                u = _dot(h_scr[slot][...],wu_ref[:, col:col + FF_CHUNK])
                cw = cw_ref[:, col:col + FF_CHUNK]
                cb = cb_ref[:, col:col + FF_CHUNK]
                per_stream = []
                for s in range(s_blk):
                    us = u[s * sub:(s + 1) * sub]
                    per_stream.append(_conv3(us, hist[col][s], cw) + cb)
                    hist[col][s] = us[sub - A_HEAD:]
                halves.append(per_stream[0] if s_blk == 1 else jnp.concatenate(per_stream, axis=0))
            val, gate = halves
            act_scr[slot][:, j * FF_CHUNK:(j + 1) * FF_CHUNK] = (_silu(gate) * val).astype(BF16)
            yield

    def stage_down(i):
        slot, r0 = i % n_slots, i * sub
        for c in range(0, D_MODEL, PROJ_COLS):
            x = x_ref[:, r0:r0 + sub, c:c + PROJ_COLS].reshape(rows, PROJ_COLS)
            out = x + _dot(act_scr[slot][...],wd_ref[:, c:c + PROJ_COLS])
            out_ref[:, r0:r0 + sub, c:c + PROJ_COLS] = out.reshape(s_blk, sub, PROJ_COLS)
            yield
        if final:
            out = out_ref[:, r0:r0 + sub, :].reshape(rows, D_MODEL)
            out_ref[:, r0:r0 + sub, :] = _rms(out, gf_ref[...]).reshape(s_blk, sub, D_MODEL)
            yield

    _interleave([stage_gated(0)])
    for i in range(1, n_sub):
        _interleave([stage_gated(i), stage_down(i - 1)], every=(1, DOWN_EVERY))
    _interleave([stage_down(n_sub - 1)])

    for c in cols:
        for s in range(s_blk):
            new_f_ref[s, :, c:c + FF_CHUNK] = hist[c][s]
            carry[s, :, c:c + FF_CHUNK] = hist[c][s]


def _ffn(layer, x, prev_f, weights, final_g, *, prompt, final):
    (g2, wu, cw, cb, wd) = weights
    nb, seq, _ = x.shape
    if prompt:
        s_blk, t_blk = 1, FFN_TILE
    else:
        s_blk, t_blk = SAMPLE_STREAMS, seq
    grid = (nb // s_blk, seq // t_blk)
    sub_rows = s_blk * min(t_blk, SUB_TILE)
    n_slots = min(2, s_blk * t_blk // sub_rows)
    tile_spec = pl.BlockSpec((s_blk, t_blk, D_MODEL), lambda b, t: (b, t, 0))
    hist_spec = pl.BlockSpec((s_blk, A_HEAD, 2 * D_FF), lambda b, t: (b, 0, 0))
    prev_spec = pl.BlockSpec((None, s_blk, A_HEAD, 2 * D_FF), lambda b, t: (layer, b, 0, 0))
    consts = [g2, wu, cw, cb, wd]
    return pl.pallas_call(
        functools.partial(_ffn_body, s_blk=s_blk, t_blk=t_blk, final=final),
        grid=grid,
        in_specs=[tile_spec, prev_spec] + [_layer_spec(layer, c.shape) for c in consts] + [_const_spec(final_g.shape)],
        out_specs=[tile_spec, hist_spec],
        out_shape=[jax.ShapeDtypeStruct((nb, seq, D_MODEL), F32),
                   jax.ShapeDtypeStruct((nb, A_HEAD, 2 * D_FF), F32)],
        scratch_shapes=([pltpu.VMEM((s_blk, A_HEAD, 2 * D_FF), F32)]
                        + [pltpu.VMEM((sub_rows, D_MODEL), BF16)] * n_slots
                        + [pltpu.VMEM((sub_rows, D_FF), BF16)] * n_slots),
        compiler_params=pltpu.CompilerParams(dimension_semantics=("arbitrary", "arbitrary"),
                                             vmem_limit_bytes=VMEM_LIMIT),
        name="ffn_prompt" if prompt else "ffn_sample",
    )(x, prev_f, *consts, final_g)


def _rel_bias_toeplitz(rel_bias_l, n_q, n_k):
    lo = ATTN_WINDOW - (n_k - 1) + REL_CLIP
    hi = ATTN_WINDOW + (n_q - 1) + REL_CLIP
    pad_lo, pad_hi = max(0, -lo), max(0, hi - 2 * REL_CLIP)
    ext = jnp.pad(rel_bias_l, ((0, 0), (pad_lo, pad_hi)), mode="edge")[:, lo + pad_lo:hi + pad_lo + 1]
    period = n_q + n_k
    w = jnp.pad(ext[:, ::-1], ((0, 0), (0, 1)))
    flat = jnp.tile(w, (1, n_q))[:, :n_q * (period - 1)]
    return flat.reshape(-1, n_q, period - 1)[:, :, n_q - 1:n_q - 1 + n_k].astype(F32)


def _prompt_bias(rel_bias):
    r = np.arange(Q_PAIR)[:, None]
    kk = np.arange(K_PAIR)[None, :] - CHUNK * (r // CHUNK)
    in_band = (kk >= 0) & (kk < ATTN_WINDOW + CHUNK)
    depth = rel_bias.shape[0]
    tab = _rel_bias_toeplitz(rel_bias.reshape(depth * N_HEADS, -1), Q_PAIR, K_PAIR)
    return jnp.where(in_band[None], tab * LOG2E, NEG_INF).reshape(depth, N_HEADS * Q_PAIR, K_PAIR)


def _sample_bias(rel_bias, t):
    depth = rel_bias.shape[0]
    tab = _rel_bias_toeplitz(rel_bias.reshape(depth * N_HEADS, -1), t, ATTN_WINDOW + t)
    tab = tab.reshape(depth, N_HEADS * t, ATTN_WINDOW + t) * LOG2E
    return tab[:, :, :ATTN_WINDOW], tab[:, :, ATTN_WINDOW:]


def _pad_history(prev, head):
    return jnp.pad(prev, ((0, 0), (0, 0), (head - prev.shape[2], 0), (0, 0)))


def kernel(x_prompt, x_sample, cache_conv_a, cache_conv_b, cache_ffn_conv, cache_attn_k, cache_attn_v, cache_mem_k,
           cache_mem_v, mem_prompt, norm1_g, w_in, w_mem_kv, conv_a_w, conv_b_w, conv_b_bias, ln_b_g, ln_b_b, rel_bias,
           grp_norm_g, w_out, norm2_g, w_up, ffn_conv_w, ffn_conv_b, w_down, final_g):
    depth = w_in.shape[0]
    bp, tp, _ = x_prompt.shape
    bs, ts, _ = x_sample.shape
    assert tp % PROMPT_TILE == 0 and tp % FFN_TILE == 0 and SUB_TILE % Q_PAIR == 0 and tp >= ATTN_WINDOW
    assert bs % SAMPLE_STREAMS == 0 and ts % SUBLANES == 0 and ts >= B_HEAD
    assert cache_attn_k.shape[2] == ATTN_WINDOW

    row = lambda p: p.reshape(depth, 1, -1)
    mixer_w = (row(norm1_g), w_in.astype(BF16), conv_a_w, conv_b_w, row(conv_b_bias), row(ln_b_g), row(ln_b_b),
               row(grp_norm_g), w_out.astype(BF16))
    ffn_w = (row(norm2_g), w_up.astype(BF16), ffn_conv_w, row(ffn_conv_b), w_down.astype(BF16))
    final_row = final_g.reshape(1, D_MODEL)
    mem_k_all, mem_v_all = _memkv(mem_prompt, w_mem_kv.astype(BF16))

    prompt_bias = (_prompt_bias(rel_bias),)
    sample_bias = _sample_bias(rel_bias, ts)
    zero_a = jnp.zeros((depth, bp, A_HEAD, G), F32)
    zero_b = jnp.zeros((depth, bp, B_HEAD, G), F32)
    zero_f = jnp.zeros((depth, bp, A_HEAD, 2 * D_FF), F32)
    hist_a = _pad_history(cache_conv_a, A_HEAD)
    hist_b = _pad_history(cache_conv_b, B_HEAD)
    hist_f = _pad_history(cache_ffn_conv, A_HEAD)
    mem_k_s = cache_mem_k.reshape(depth, bs, N_MEM, G)
    mem_v_s = cache_mem_v.reshape(depth, bs, N_MEM, G)
    attn_k_s = cache_attn_k.reshape(depth, bs, ATTN_WINDOW, G)
    attn_v_s = cache_attn_v.reshape(depth, bs, ATTN_WINDOW, G)

    xp, xs = x_prompt, x_sample
    outs = {name: [] for name in ("pa", "pb", "pf", "pk", "pv", "sa", "sb", "sf", "sk", "sv")}
    for l in range(depth):
        final = l == depth - 1
        xp, na, nb, k_p, v_p = _mixer(l, xp, mem_k_all, mem_v_all, zero_a, zero_b, None, None, mixer_w,
                                      prompt_bias, prompt=True)
        xp, nf = _ffn(l, xp, zero_f, ffn_w, final_row, prompt=True, final=final)
        outs["pa"].append(na[:, A_HEAD - (CONV_A_WIDTH - 1):])
        outs["pb"].append(nb[:, B_HEAD - (CONV_B_WIDTH - 1):])
        outs["pf"].append(nf[:, A_HEAD - (FFN_CONV_WIDTH - 1):])
        outs["pk"].append(k_p.reshape(bp, ATTN_WINDOW, N_HEADS, HEAD_DIM))
        outs["pv"].append(v_p.reshape(bp, ATTN_WINDOW, N_HEADS, HEAD_DIM))

        xs, na, nb, k_s, v_s = _mixer(l, xs, mem_k_s, mem_v_s, hist_a, hist_b, attn_k_s, attn_v_s, mixer_w,
                                      sample_bias, prompt=False)
        xs, nf = _ffn(l, xs, hist_f, ffn_w, final_row, prompt=False, final=final)
        outs["sa"].append(na[:, A_HEAD - (CONV_A_WIDTH - 1):])
        outs["sb"].append(nb[:, B_HEAD - (CONV_B_WIDTH - 1):])
        outs["sf"].append(nf[:, A_HEAD - (FFN_CONV_WIDTH - 1):])
        outs["sk"].append(k_s.reshape(bs, ts, N_HEADS, HEAD_DIM))
        outs["sv"].append(v_s.reshape(bs, ts, N_HEADS, HEAD_DIM))

    st = {name: jnp.stack(v) for name, v in outs.items()}
    mem_shape = (depth, bp, N_MEM, N_HEADS, HEAD_DIM)
    return (xp, xs, st["pa"], st["pb"], st["pf"], st["pk"], st["pv"],
            mem_k_all.reshape(mem_shape), mem_v_all.reshape(mem_shape),
            st["sa"], st["sb"], st["sf"], st["sk"], st["sv"])
```

```python
import functools

import numpy as np
import jax
import jax.numpy as jnp
from jax import lax
from jax.experimental import pallas as pl
from jax.experimental.pallas import tpu as pltpu

F32 = jnp.float32
BF16 = jnp.bfloat16

D_MODEL = 1024
G = 256
N_HEADS = 4
HEAD_DIM = 64
CHUNK = 64
N_MEM = 256
N_PREV_CHUNKS = 8
ATTN_WINDOW = N_PREV_CHUNKS * CHUNK
REL_CLIP = 128
CONV_A_WIDTH = 3
CONV_B_WIDTH = 31
FFN_CONV_WIDTH = 3
D_FF = 2816
D_IN_PROJ = 9 * G
RMS_EPS = 1e-6
LN_EPS = 1e-5
NEG_INF = -1e30
QK_SCALE = HEAD_DIM ** -0.5
LOG2E = 1.4426950408889634
Q_SCALE = QK_SCALE * LOG2E

SUBLANES = 8
A_HEAD = SUBLANES
B_HEAD = 32
Q_PAIR = 2 * CHUNK
K_PAIR = ATTN_WINDOW + Q_PAIR
FF_CHUNK = 256
N_FF_CHUNKS = D_FF // FF_CHUNK
VMEM_LIMIT = 56 * 1024 * 1024

PROMPT_TILE = 1024
FFN_TILE = 1024
SUB_TILE = 256
PROJ_COLS = 256
CONV_B_ROWS = 32
OUT_EVERY = 2
DOWN_EVERY = 3
SAMPLE_STREAMS = 8
MEM_STREAMS = 4


def _rms(x, g):
    return x * lax.rsqrt(jnp.mean(x * x, axis=-1, keepdims=True) + RMS_EPS) * g


def _dot(a, b):
    return jnp.dot(a, b, preferred_element_type=F32)


def _dot_nt(a, b):
    return lax.dot_general(a, b, (((1,), (1,)), ((), ())), preferred_element_type=F32)


def _silu(x):
    return x * jax.nn.sigmoid(x)


def _interleave(stages, every=None):
    every = every or (1,) * len(stages)
    live = list(range(len(stages)))
    rounds = 0
    while live:
        for k in list(live):
            if len(live) == 1 or rounds % every[k] == every[k] - 1:
                try:
                    next(stages[k])
                except StopIteration:
                    live.remove(k)
        rounds += 1


def _roll_rows(x, shift):
    shift = shift % x.shape[0]
    return x if shift == 0 else pltpu.roll(x, shift, axis=0)


def _shift_rows(u, hist, s):
    moved = _roll_rows(u, s)
    sub = lax.broadcasted_iota(jnp.int32, (SUBLANES, u.shape[1]), 0)
    top = jnp.where(sub < s, _roll_rows(hist, s), moved[0:SUBLANES])
    return jnp.concatenate([top, moved[SUBLANES:]], axis=0)


def _conv3(u, hist, w):
    return w[2:3] * u + w[1:2] * _shift_rows(u, hist, 1) + w[0:1] * _shift_rows(u, hist, 2)


def _memkv_body(mem_ref, w_ref, mk_ref, mv_ref):
    nb = mem_ref.shape[0]
    kv = _dot(mem_ref[...].reshape(nb * N_MEM, D_MODEL).astype(BF16), w_ref[0]).reshape(nb, N_MEM, 2 * G)
    mk_ref[0] = kv[:, :, :G]
    mv_ref[0] = kv[:, :, G:]


def _memkv(mem_prompt, w_mem_kv_bf):
    depth = w_mem_kv_bf.shape[0]
    b = mem_prompt.shape[0]
    nb = MEM_STREAMS if b % MEM_STREAMS == 0 else 1
    out = jax.ShapeDtypeStruct((depth, b, N_MEM, G), F32)
    return pl.pallas_call(
        _memkv_body,
        grid=(b // nb, depth),
        in_specs=[pl.BlockSpec((nb, N_MEM, D_MODEL), lambda i, l: (i, 0, 0)),
                  pl.BlockSpec((1, D_MODEL, 2 * G), lambda i, l: (l, 0, 0))],
        out_specs=[pl.BlockSpec((1, nb, N_MEM, G), lambda i, l: (l, i, 0, 0)),
                   pl.BlockSpec((1, nb, N_MEM, G), lambda i, l: (l, i, 0, 0))],
        out_shape=[out, out],
        compiler_params=pltpu.CompilerParams(dimension_semantics=("arbitrary", "arbitrary"),
                                             vmem_limit_bytes=VMEM_LIMIT),
        name="memkv",
    )(mem_prompt, w_mem_kv_bf)


def _softmax_parts(s_list):
    m = s_list[0].max(axis=-1, keepdims=True)
    for s in s_list[1:]:
        m = jnp.maximum(m, s.max(axis=-1, keepdims=True))
    e_list = [jnp.exp2(s - m) for s in s_list]
    l = e_list[0].sum(axis=-1, keepdims=True)
    for e in e_list[1:]:
        l = l + e.sum(axis=-1, keepdims=True)
    return e_list, 1.0 / l


def _stack_heads(q):
    head = lax.broadcasted_iota(jnp.int32, q.shape, 1) // HEAD_DIM
    return jnp.concatenate([jnp.where(head == hd, q, 0.0) for hd in range(N_HEADS)], axis=0).astype(BF16)


def _unstack_heads(o):
    t = o.shape[0] // N_HEADS
    head = lax.broadcasted_iota(jnp.int32, (t, G), 1) // HEAD_DIM
    out = o[0:t]
    for hd in range(1, N_HEADS):
        out = jnp.where(head == hd, o[hd * t:(hd + 1) * t], out)
    return out


def _mixer_body(*refs, prompt, s_blk, t_blk, n_slots):
    if prompt:
        (x_ref, mk_ref, mv_ref, prev_a_ref, prev_b_ref, g1_ref, w_in_ref, caw_ref, cbw_ref, cbb_ref,
         lng_ref, lnb_ref, bias_ref, gng_ref, w_out_ref,
         x1_ref, new_a_ref, new_b_ref, k_out_ref, v_out_ref,
         hist_a, hist_b, *slots, kbuf, vbuf) = refs
    else:
        (x_ref, mk_ref, mv_ref, prev_a_ref, prev_b_ref, ck_ref, cv_ref, g1_ref, w_in_ref, caw_ref, cbw_ref,
         cbb_ref, lng_ref, lnb_ref, bias_ref, bias2_ref, gng_ref, w_out_ref,
         x1_ref, new_a_ref, new_b_ref, k_out_ref, v_out_ref,
         hist_a, hist_b, *slots) = refs
    h_scr, z_scr, y_scr = slots[:n_slots], slots[n_slots:2 * n_slots], slots[2 * n_slots:]
    sub = min(t_blk, SUB_TILE)
    n_sub = t_blk // sub
    rows = s_blk * sub
    t = pl.program_id(1)

    def group_norm(y, gi):
        return _rms(y, gng_ref[:, gi * G:(gi + 1) * G]).astype(BF16)

    @pl.when(t == 0)
    def _():
        hist_a[...] = prev_a_ref[...]
        hist_b[...] = prev_b_ref[...]

    ha = [hist_a[s] for s in range(s_blk)]
    hb = [hist_b[s] for s in range(s_blk)]

    def stage_in(i):
        slot, r0 = i % n_slots, i * sub
        x = x_ref[:, r0:r0 + sub, :].reshape(rows, D_MODEL)
        h_scr[slot][...] = _rms(x, g1_ref[...]).astype(BF16)
        yield
        for c in range(0, D_IN_PROJ, PROJ_COLS):
            z_scr[slot][:, c:c + PROJ_COLS] = _dot(h_scr[slot][...],w_in_ref[:, c:c + PROJ_COLS])
            yield

    def stage_conv(i):
        slot = i % n_slots
        caw = caw_ref[...]
        for s in range(s_blk):
            sl = slice(s * sub, (s + 1) * sub)
            p = z_scr[slot][sl, G:2 * G] * z_scr[slot][sl, 2 * G:3 * G]
            y_scr[slot][sl, 0:G] = group_norm(z_scr[slot][sl, 0:G] * _conv3(p, ha[s], caw), 0)
            ha[s] = p[sub - A_HEAD:]
        yield
        cbw = cbw_ref[...]
        first = B_HEAD - (CONV_B_WIDTH - 1)
        r_step = min(sub, CONV_B_ROWS)
        for s in range(s_blk):
            sl = slice(s * sub, (s + 1) * sub)
            glu = z_scr[slot][sl, 3 * G:4 * G] * jax.nn.sigmoid(z_scr[slot][sl, 4 * G:5 * G])
            xs = jnp.concatenate([hb[s], glu], axis=0)
            hb[s] = glu[sub - B_HEAD:]
            moved = [_roll_rows(xs, -b) for b in range(SUBLANES)]
            if s_blk == 1:
                yield
            for q0 in range(0, sub, r_step):
                acc = None
                for j in range(CONV_B_WIDTH):
                    off = first + j
                    lo = q0 + (off // SUBLANES) * SUBLANES
                    term = cbw[j:j + 1] * moved[off % SUBLANES][lo:lo + r_step]
                    acc = term if acc is None else acc + term
                c = acc + cbb_ref[...]
                cc = c - jnp.mean(c, axis=-1, keepdims=True)
                c = cc * lax.rsqrt(jnp.mean(cc * cc, axis=-1, keepdims=True) + LN_EPS) * lng_ref[...] + lnb_ref[...]
                y_scr[slot][s * sub + q0:s * sub + q0 + r_step, G:2 * G] = group_norm(_silu(c), 1)
                if s_blk == 1:
                    yield
            if s_blk > 1 and s % 2 == 1:
                yield

    def stage_attn(i):
        slot, r0 = i % n_slots, i * sub
        k_new = z_scr[slot][:, 6 * G:7 * G]
        v_new = z_scr[slot][:, 7 * G:8 * G]
        kv_off = t_blk - k_out_ref.shape[1]
        if r0 >= kv_off:
            k_out_ref[:, r0 - kv_off:r0 - kv_off + sub, :] = k_new.reshape(s_blk, sub, G)
            v_out_ref[:, r0 - kv_off:r0 - kv_off + sub, :] = v_new.reshape(s_blk, sub, G)
        if not prompt:
            for j in range(s_blk):
                sl = slice(j * sub, (j + 1) * sub)
                qs = _stack_heads(z_scr[slot][sl, 5 * G:6 * G] * Q_SCALE)
                kn = z_scr[slot][sl, 6 * G:7 * G].astype(BF16)
                vn = z_scr[slot][sl, 7 * G:8 * G].astype(BF16)
                s_old = _dot_nt(qs, ck_ref[j].astype(BF16)) + bias_ref[...]
                s_new = _dot_nt(qs, kn) + bias2_ref[...]
                (e_old, e_new), inv = _softmax_parts([s_old, s_new])
                o = (_dot(e_old.astype(BF16), cv_ref[j].astype(BF16)) + _dot(e_new.astype(BF16), vn)) * inv
                y_scr[slot][sl, 2 * G:3 * G] = group_norm(_unstack_heads(o), 2)
                qs = _stack_heads(z_scr[slot][sl, 8 * G:9 * G] * Q_SCALE)
                (e,), inv = _softmax_parts([_dot_nt(qs, mk_ref[j].astype(BF16))])
                o = _dot(e.astype(BF16), mv_ref[j].astype(BF16)) * inv
                y_scr[slot][sl, 3 * G:4 * G] = group_norm(_unstack_heads(o), 3)
                yield
            return

        if i == 0:
            @pl.when(t == 0)
            def _():
                kbuf[0:ATTN_WINDOW, :] = jnp.zeros((ATTN_WINDOW, G), BF16)
                vbuf[0:ATTN_WINDOW, :] = jnp.zeros((ATTN_WINDOW, G), BF16)

        row0 = pl.multiple_of(t * t_blk + r0, sub)
        kbuf[pl.ds(ATTN_WINDOW + row0, sub), :] = k_new.astype(BF16)
        vbuf[pl.ds(ATTN_WINDOW + row0, sub), :] = v_new.astype(BF16)
        yield
        for lo in range(0, sub, Q_PAIR):
            base = pl.multiple_of(row0 + lo, Q_PAIR)
            qs = _stack_heads(z_scr[slot][lo:lo + Q_PAIR, 5 * G:6 * G] * Q_SCALE)
            sc = _dot_nt(qs, kbuf[pl.ds(base, K_PAIR), :]) + bias_ref[...]
            if r0 + lo < ATTN_WINDOW:
                key_ok = (lax.broadcasted_iota(jnp.int32, (1, K_PAIR), 1) + base) >= ATTN_WINDOW
                sc = jnp.where(key_ok, sc, NEG_INF)
            (e,), inv = _softmax_parts([sc])
            yield
            o = _dot(e.astype(BF16), vbuf[pl.ds(base, K_PAIR), :]) * inv
            y_scr[slot][lo:lo + Q_PAIR, 2 * G:3 * G] = group_norm(_unstack_heads(o), 2)
            yield
            qs = _stack_heads(z_scr[slot][lo:lo + Q_PAIR, 8 * G:9 * G] * Q_SCALE)
            (e,), inv = _softmax_parts([_dot_nt(qs, mk_ref[0].astype(BF16))])
            yield
            o = _dot(e.astype(BF16), mv_ref[0].astype(BF16)) * inv
            y_scr[slot][lo:lo + Q_PAIR, 3 * G:4 * G] = group_norm(_unstack_heads(o), 3)
            yield

    def stage_out(i):
        slot, r0 = i % n_slots, i * sub
        for c in range(0, D_MODEL, PROJ_COLS):
            x = x_ref[:, r0:r0 + sub, c:c + PROJ_COLS].reshape(rows, PROJ_COLS)
            x1 = x + _dot(y_scr[slot][...],w_out_ref[:, c:c + PROJ_COLS])
            x1_ref[:, r0:r0 + sub, c:c + PROJ_COLS] = x1.reshape(s_blk, sub, PROJ_COLS)
            yield

    _interleave([stage_in(0)])
    for i in range(n_sub):
        stages, every = [stage_conv(i), stage_attn(i)], [1, 1]
        if i + 1 < n_sub:
            stages.append(stage_in(i + 1))
            every.append(1)
        if i > 0:
            stages.append(stage_out(i - 1))
            every.append(OUT_EVERY)
        _interleave(stages, every)
    _interleave([stage_out(n_sub - 1)])
    for s in range(s_blk):
        new_a_ref[s] = ha[s]
        new_b_ref[s] = hb[s]
        hist_a[s] = ha[s]
        hist_b[s] = hb[s]


def _const_spec(shape):
    zeros = (0,) * len(shape)
    return pl.BlockSpec(shape, lambda b, t: zeros)


def _layer_spec(layer, shape):
    zeros = (0,) * (len(shape) - 1)
    return pl.BlockSpec((None,) + tuple(shape[1:]), lambda b, t: (layer,) + zeros, pipeline_mode=pl.Buffered(1))


def _mixer(layer, x, mk, mv, prev_a, prev_b, cache_k, cache_v, weights, bias_tables, *, prompt):
    (g1, w_in, caw, cbw, cbb, lng, lnb, gng, w_out) = weights
    nb, seq, _ = x.shape
    if prompt:
        s_blk, t_blk = 1, PROMPT_TILE
    else:
        s_blk, t_blk = SAMPLE_STREAMS, seq
    grid = (nb // s_blk, seq // t_blk)
    nt = grid[1]
    rows = s_blk * t_blk

    stream_spec = lambda r, c: pl.BlockSpec((s_blk, r, c), lambda b, t: (b, 0, 0))
    layer_stream_spec = lambda r, c: pl.BlockSpec((None, s_blk, r, c), lambda b, t: (layer, b, 0, 0))
    tile_spec = lambda c: pl.BlockSpec((s_blk, t_blk, c), lambda b, t: (b, t, 0))
    in_specs = [tile_spec(D_MODEL), layer_stream_spec(N_MEM, G), layer_stream_spec(N_MEM, G),
                layer_stream_spec(A_HEAD, G), layer_stream_spec(B_HEAD, G)]
    args = [x, mk, mv, prev_a, prev_b]
    if not prompt:
        in_specs += [layer_stream_spec(ATTN_WINDOW, G), layer_stream_spec(ATTN_WINDOW, G)]
        args += [cache_k, cache_v]
    consts = [g1, w_in, caw, cbw, cbb, lng, lnb, *bias_tables, gng, w_out]
    in_specs += [_layer_spec(layer, c.shape) for c in consts]
    args += consts

    if prompt:
        kv_blk = min(t_blk, ATTN_WINDOW)
        first_kept = nt - ATTN_WINDOW // kv_blk
        kv_spec = pl.BlockSpec((1, kv_blk, G), lambda b, t: (b, jnp.maximum(t - first_kept, 0), 0))
        kv_rows = ATTN_WINDOW
    else:
        kv_spec = tile_spec(G)
        kv_rows = seq
    out_specs = [tile_spec(D_MODEL), stream_spec(A_HEAD, G), stream_spec(B_HEAD, G), kv_spec, kv_spec]
    out_shape = [jax.ShapeDtypeStruct((nb, seq, D_MODEL), F32),
                 jax.ShapeDtypeStruct((nb, A_HEAD, G), F32),
                 jax.ShapeDtypeStruct((nb, B_HEAD, G), F32),
                 jax.ShapeDtypeStruct((nb, kv_rows, G), F32),
                 jax.ShapeDtypeStruct((nb, kv_rows, G), F32)]
    sub_rows = s_blk * min(t_blk, SUB_TILE)
    n_slots = min(2, rows // sub_rows)
    scratch = ([pltpu.VMEM((s_blk, A_HEAD, G), F32), pltpu.VMEM((s_blk, B_HEAD, G), F32)]
               + [pltpu.VMEM((sub_rows, D_MODEL), BF16)] * n_slots
               + [pltpu.VMEM((sub_rows, D_IN_PROJ), F32)] * n_slots
               + [pltpu.VMEM((sub_rows, 4 * G), BF16)] * n_slots)
    if prompt:
        scratch += [pltpu.VMEM((ATTN_WINDOW + seq, G), BF16), pltpu.VMEM((ATTN_WINDOW + seq, G), BF16)]
    return pl.pallas_call(
        functools.partial(_mixer_body, prompt=prompt, s_blk=s_blk, t_blk=t_blk, n_slots=n_slots),
        grid=grid, in_specs=in_specs, out_specs=out_specs, out_shape=out_shape, scratch_shapes=scratch,
        compiler_params=pltpu.CompilerParams(dimension_semantics=("arbitrary", "arbitrary"),
                                             vmem_limit_bytes=VMEM_LIMIT),
        name="mixer_prompt" if prompt else "mixer_sample",
    )(*args)


def _ffn_body(x_ref, prev_ref, g2_ref, wu_ref, cw_ref, cb_ref, wd_ref, gf_ref,
              out_ref, new_f_ref, carry, *slots, s_blk, t_blk, final):
    sub = min(t_blk, SUB_TILE)
    n_sub = t_blk // sub
    n_slots = len(slots) // 2
    h_scr, act_scr = slots[:n_slots], slots[n_slots:]
    rows = s_blk * sub
    t = pl.program_id(1)

    @pl.when(t == 0)
    def _():
        carry[...] = prev_ref[...]

    cols = [c for j in range(N_FF_CHUNKS) for c in (j * FF_CHUNK, D_FF + j * FF_CHUNK)]
    hist = {c: [carry[s, :, c:c + FF_CHUNK] for s in range(s_blk)] for c in cols}

    def stage_gated(i):
        slot, r0 = i % n_slots, i * sub
        x = x_ref[:, r0:r0 + sub, :].reshape(rows, D_MODEL)
        h_scr[slot][...] = _rms(x, g2_ref[...]).astype(BF16)
        yield
        for j in range(N_FF_CHUNKS):
            halves = []
            for col in (j * FF_CHUNK, D_FF + j * FF_CHUNK):
                u = _dot(h_scr[slot][...],wu_ref[:, col:col + FF_CHUNK])
                cw = cw_ref[:, col:col + FF_CHUNK]
                cb = cb_ref[:, col:col + FF_CHUNK]
                per_stream = []
                for s in range(s_blk):
                    us = u[s * sub:(s + 1) * sub]
                    per_stream.append(_conv3(us, hist[col][s], cw) + cb)
                    hist[col][s] = us[sub - A_HEAD:]
                halves.append(per_stream[0] if s_blk == 1 else jnp.concatenate(per_stream, axis=0))
            val, gate = halves
            act_scr[slot][:, j * FF_CHUNK:(j + 1) * FF_CHUNK] = (_silu(gate) * val).astype(BF16)
            yield

    def stage_down(i):
        slot, r0 = i % n_slots, i * sub
        for c in range(0, D_MODEL, PROJ_COLS):
            x = x_ref[:, r0:r0 + sub, c:c + PROJ_COLS].reshape(rows, PROJ_COLS)
            out = x + _dot(act_scr[slot][...],wd_ref[:, c:c + PROJ_COLS])
            out_ref[:, r0:r0 + sub, c:c + PROJ_COLS] = out.reshape(s_blk, sub, PROJ_COLS)
            yield
        if final:
            out = out_ref[:, r0:r0 + sub, :].reshape(rows, D_MODEL)
            out_ref[:, r0:r0 + sub, :] = _rms(out, gf_ref[...]).reshape(s_blk, sub, D_MODEL)
            yield

    _interleave([stage_gated(0)])
    for i in range(1, n_sub):
        _interleave([stage_gated(i), stage_down(i - 1)], every=(1, DOWN_EVERY))
    _interleave([stage_down(n_sub - 1)])

    for c in cols:
        for s in range(s_blk):
            new_f_ref[s, :, c:c + FF_CHUNK] = hist[c][s]
            carry[s, :, c:c + FF_CHUNK] = hist[c][s]


def _ffn(layer, x, prev_f, weights, final_g, *, prompt, final):
    (g2, wu, cw, cb, wd) = weights
    nb, seq, _ = x.shape
    if prompt:
        s_blk, t_blk = 1, FFN_TILE
    else:
        s_blk, t_blk = SAMPLE_STREAMS, seq
    grid = (nb // s_blk, seq // t_blk)
    sub_rows = s_blk * min(t_blk, SUB_TILE)
    n_slots = min(2, s_blk * t_blk // sub_rows)
    tile_spec = pl.BlockSpec((s_blk, t_blk, D_MODEL), lambda b, t: (b, t, 0))
    hist_spec = pl.BlockSpec((s_blk, A_HEAD, 2 * D_FF), lambda b, t: (b, 0, 0))
    prev_spec = pl.BlockSpec((None, s_blk, A_HEAD, 2 * D_FF), lambda b, t: (layer, b, 0, 0))
    consts = [g2, wu, cw, cb, wd]
    return pl.pallas_call(
        functools.partial(_ffn_body, s_blk=s_blk, t_blk=t_blk, final=final),
        grid=grid,
        in_specs=[tile_spec, prev_spec] + [_layer_spec(layer, c.shape) for c in consts] + [_const_spec(final_g.shape)],
        out_specs=[tile_spec, hist_spec],
        out_shape=[jax.ShapeDtypeStruct((nb, seq, D_MODEL), F32),
                   jax.ShapeDtypeStruct((nb, A_HEAD, 2 * D_FF), F32)],
        scratch_shapes=([pltpu.VMEM((s_blk, A_HEAD, 2 * D_FF), F32)]
                        + [pltpu.VMEM((sub_rows, D_MODEL), BF16)] * n_slots
                        + [pltpu.VMEM((sub_rows, D_FF), BF16)] * n_slots),
        compiler_params=pltpu.CompilerParams(dimension_semantics=("arbitrary", "arbitrary"),
                                             vmem_limit_bytes=VMEM_LIMIT),
        name="ffn_prompt" if prompt else "ffn_sample",
    )(x, prev_f, *consts, final_g)


def _rel_bias_toeplitz(rel_bias_l, n_q, n_k):
    lo = ATTN_WINDOW - (n_k - 1) + REL_CLIP
    hi = ATTN_WINDOW + (n_q - 1) + REL_CLIP
    pad_lo, pad_hi = max(0, -lo), max(0, hi - 2 * REL_CLIP)
    ext = jnp.pad(rel_bias_l, ((0, 0), (pad_lo, pad_hi)), mode="edge")[:, lo + pad_lo:hi + pad_lo + 1]
    period = n_q + n_k
    w = jnp.pad(ext[:, ::-1], ((0, 0), (0, 1)))
    flat = jnp.tile(w, (1, n_q))[:, :n_q * (period - 1)]
    return flat.reshape(-1, n_q, period - 1)[:, :, n_q - 1:n_q - 1 + n_k].astype(F32)


def _prompt_bias(rel_bias):
    r = np.arange(Q_PAIR)[:, None]
    kk = np.arange(K_PAIR)[None, :] - CHUNK * (r // CHUNK)
    in_band = (kk >= 0) & (kk < ATTN_WINDOW + CHUNK)
    depth = rel_bias.shape[0]
    tab = _rel_bias_toeplitz(rel_bias.reshape(depth * N_HEADS, -1), Q_PAIR, K_PAIR)
    return jnp.where(in_band[None], tab * LOG2E, NEG_INF).reshape(depth, N_HEADS * Q_PAIR, K_PAIR)


def _sample_bias(rel_bias, t):
    depth = rel_bias.shape[0]
    tab = _rel_bias_toeplitz(rel_bias.reshape(depth * N_HEADS, -1), t, ATTN_WINDOW + t)
    tab = tab.reshape(depth, N_HEADS * t, ATTN_WINDOW + t) * LOG2E
    return tab[:, :, :ATTN_WINDOW], tab[:, :, ATTN_WINDOW:]


def _pad_history(prev, head):
    return jnp.pad(prev, ((0, 0), (0, 0), (head - prev.shape[2], 0), (0, 0)))


def kernel(x_prompt, x_sample, cache_conv_a, cache_conv_b, cache_ffn_conv, cache_attn_k, cache_attn_v, cache_mem_k,
           cache_mem_v, mem_prompt, norm1_g, w_in, w_mem_kv, conv_a_w, conv_b_w, conv_b_bias, ln_b_g, ln_b_b, rel_bias,
           grp_norm_g, w_out, norm2_g, w_up, ffn_conv_w, ffn_conv_b, w_down, final_g):
    depth = w_in.shape[0]
    bp, tp, _ = x_prompt.shape
    bs, ts, _ = x_sample.shape
    assert tp % PROMPT_TILE == 0 and tp % FFN_TILE == 0 and SUB_TILE % Q_PAIR == 0 and tp >= ATTN_WINDOW
    assert bs % SAMPLE_STREAMS == 0 and ts % SUBLANES == 0 and ts >= B_HEAD
    assert cache_attn_k.shape[2] == ATTN_WINDOW

    row = lambda p: p.reshape(depth, 1, -1)
    mixer_w = (row(norm1_g), w_in.astype(BF16), conv_a_w, conv_b_w, row(conv_b_bias), row(ln_b_g), row(ln_b_b),
               row(grp_norm_g), w_out.astype(BF16))
    ffn_w = (row(norm2_g), w_up.astype(BF16), ffn_conv_w, row(ffn_conv_b), w_down.astype(BF16))
    final_row = final_g.reshape(1, D_MODEL)
    mem_k_all, mem_v_all = _memkv(mem_prompt, w_mem_kv.astype(BF16))

    prompt_bias = (_prompt_bias(rel_bias),)
    sample_bias = _sample_bias(rel_bias, ts)
    zero_a = jnp.zeros((depth, bp, A_HEAD, G), F32)
    zero_b = jnp.zeros((depth, bp, B_HEAD, G), F32)
    zero_f = jnp.zeros((depth, bp, A_HEAD, 2 * D_FF), F32)
    hist_a = _pad_history(cache_conv_a, A_HEAD)
    hist_b = _pad_history(cache_conv_b, B_HEAD)
    hist_f = _pad_history(cache_ffn_conv, A_HEAD)
    mem_k_s = cache_mem_k.reshape(depth, bs, N_MEM, G)
    mem_v_s = cache_mem_v.reshape(depth, bs, N_MEM, G)
    attn_k_s = cache_attn_k.reshape(depth, bs, ATTN_WINDOW, G)
    attn_v_s = cache_attn_v.reshape(depth, bs, ATTN_WINDOW, G)

    xp, xs = x_prompt, x_sample
    outs = {name: [] for name in ("pa", "pb", "pf", "pk", "pv", "sa", "sb", "sf", "sk", "sv")}
    for l in range(depth):
        final = l == depth - 1
        xp, na, nb, k_p, v_p = _mixer(l, xp, mem_k_all, mem_v_all, zero_a, zero_b, None, None, mixer_w,
                                      prompt_bias, prompt=True)
        xp, nf = _ffn(l, xp, zero_f, ffn_w, final_row, prompt=True, final=final)
        outs["pa"].append(na[:, A_HEAD - (CONV_A_WIDTH - 1):])
        outs["pb"].append(nb[:, B_HEAD - (CONV_B_WIDTH - 1):])
        outs["pf"].append(nf[:, A_HEAD - (FFN_CONV_WIDTH - 1):])
        outs["pk"].append(k_p.reshape(bp, ATTN_WINDOW, N_HEADS, HEAD_DIM))
        outs["pv"].append(v_p.reshape(bp, ATTN_WINDOW, N_HEADS, HEAD_DIM))

        xs, na, nb, k_s, v_s = _mixer(l, xs, mem_k_s, mem_v_s, hist_a, hist_b, attn_k_s, attn_v_s, mixer_w,
                                      sample_bias, prompt=False)
        xs, nf = _ffn(l, xs, hist_f, ffn_w, final_row, prompt=False, final=final)
        outs["sa"].append(na[:, A_HEAD - (CONV_A_WIDTH - 1):])
        outs["sb"].append(nb[:, B_HEAD - (CONV_B_WIDTH - 1):])
        outs["sf"].append(nf[:, A_HEAD - (FFN_CONV_WIDTH - 1):])
        outs["sk"].append(k_s.reshape(bs, ts, N_HEADS, HEAD_DIM))
        outs["sv"].append(v_s.reshape(bs, ts, N_HEADS, HEAD_DIM))

    st = {name: jnp.stack(v) for name, v in outs.items()}
    mem_shape = (depth, bp, N_MEM, N_HEADS, HEAD_DIM)
    return (xp, xs, st["pa"], st["pb"], st["pf"], st["pk"], st["pv"],
            mem_k_all.reshape(mem_shape), mem_v_all.reshape(mem_shape),
            st["sa"], st["sb"], st["sf"], st["sk"], st["sv"])
```

```python
import functools

import numpy as np
import jax
import jax.numpy as jnp
from jax import lax
from jax.experimental import pallas as pl
from jax.experimental.pallas import tpu as pltpu

F32 = jnp.float32
BF16 = jnp.bfloat16

D_MODEL = 1024
G = 256
N_HEADS = 4
HEAD_DIM = 64
CHUNK = 64
N_MEM = 256
N_PREV_CHUNKS = 8
ATTN_WINDOW = N_PREV_CHUNKS * CHUNK
REL_CLIP = 128
CONV_A_WIDTH = 3
CONV_B_WIDTH = 31
FFN_CONV_WIDTH = 3
D_FF = 2816
D_IN_PROJ = 9 * G
RMS_EPS = 1e-6
LN_EPS = 1e-5
NEG_INF = -1e30
QK_SCALE = HEAD_DIM ** -0.5
LOG2E = 1.4426950408889634
Q_SCALE = QK_SCALE * LOG2E

SUBLANES = 8
A_HEAD = SUBLANES
B_HEAD = 32
Q_PAIR = 2 * CHUNK
K_PAIR = ATTN_WINDOW + Q_PAIR
FF_CHUNK = 256
N_FF_CHUNKS = D_FF // FF_CHUNK
VMEM_LIMIT = 56 * 1024 * 1024

PROMPT_TILE = 1024
FFN_TILE = 512
SUB_TILE = 256
PROJ_COLS = 256
CONV_B_ROWS = 32
OUT_EVERY = 2
GATE_ROWS = 128
DOWN_EVERY = 3
SAMPLE_STREAMS = 8
MEM_STREAMS = 4


def _rms(x, g):
    return x * lax.rsqrt(jnp.mean(x * x, axis=-1, keepdims=True) + RMS_EPS) * g


def _dot(a, b):
    return jnp.dot(a, b, preferred_element_type=F32)


def _dot_nt(a, b):
    return lax.dot_general(a, b, (((1,), (1,)), ((), ())), preferred_element_type=F32)


def _silu(x):
    return x * jax.nn.sigmoid(x)


def _interleave(stages, every=None):
    every = every or (1,) * len(stages)
    live = list(range(len(stages)))
    rounds = 0
    while live:
        for k in list(live):
            if len(live) == 1 or rounds % every[k] == every[k] - 1:
                try:
                    next(stages[k])
                except StopIteration:
                    live.remove(k)
        rounds += 1


def _roll_rows(x, shift):
    shift = shift % x.shape[0]
    return x if shift == 0 else pltpu.roll(x, shift, axis=0)


def _shift_rows(u, hist, s):
    moved = _roll_rows(u, s)
    sub = lax.broadcasted_iota(jnp.int32, (SUBLANES, u.shape[1]), 0)
    top = jnp.where(sub < s, _roll_rows(hist, s), moved[0:SUBLANES])
    return jnp.concatenate([top, moved[SUBLANES:]], axis=0)


def _conv3(u, hist, w):
    return w[2:3] * u + w[1:2] * _shift_rows(u, hist, 1) + w[0:1] * _shift_rows(u, hist, 2)


def _memkv_body(mem_ref, w_ref, mk_ref, mv_ref):
    nb = mem_ref.shape[0]
    kv = _dot(mem_ref[...].reshape(nb * N_MEM, D_MODEL).astype(BF16), w_ref[0]).reshape(nb, N_MEM, 2 * G)
    mk_ref[0] = kv[:, :, :G]
    mv_ref[0] = kv[:, :, G:]


def _memkv(mem_prompt, w_mem_kv_bf):
    depth = w_mem_kv_bf.shape[0]
    b = mem_prompt.shape[0]
    nb = MEM_STREAMS if b % MEM_STREAMS == 0 else 1
    out = jax.ShapeDtypeStruct((depth, b, N_MEM, G), F32)
    return pl.pallas_call(
        _memkv_body,
        grid=(b // nb, depth),
        in_specs=[pl.BlockSpec((nb, N_MEM, D_MODEL), lambda i, l: (i, 0, 0)),
                  pl.BlockSpec((1, D_MODEL, 2 * G), lambda i, l: (l, 0, 0))],
        out_specs=[pl.BlockSpec((1, nb, N_MEM, G), lambda i, l: (l, i, 0, 0)),
                   pl.BlockSpec((1, nb, N_MEM, G), lambda i, l: (l, i, 0, 0))],
        out_shape=[out, out],
        compiler_params=pltpu.CompilerParams(dimension_semantics=("arbitrary", "arbitrary"),
                                             vmem_limit_bytes=VMEM_LIMIT),
        name="memkv",
    )(mem_prompt, w_mem_kv_bf)


def _softmax_parts(s_list):
    m = s_list[0].max(axis=-1, keepdims=True)
    for s in s_list[1:]:
        m = jnp.maximum(m, s.max(axis=-1, keepdims=True))
    e_list = [jnp.exp2(s - m) for s in s_list]
    l = e_list[0].sum(axis=-1, keepdims=True)
    for e in e_list[1:]:
        l = l + e.sum(axis=-1, keepdims=True)
    return e_list, 1.0 / l


def _stack_heads(q):
    head = lax.broadcasted_iota(jnp.int32, q.shape, 1) // HEAD_DIM
    return jnp.concatenate([jnp.where(head == hd, q, 0.0) for hd in range(N_HEADS)], axis=0).astype(BF16)


def _unstack_heads(o):
    t = o.shape[0] // N_HEADS
    head = lax.broadcasted_iota(jnp.int32, (t, G), 1) // HEAD_DIM
    out = o[0:t]
    for hd in range(1, N_HEADS):
        out = jnp.where(head == hd, o[hd * t:(hd + 1) * t], out)
    return out


def _mixer_body(*refs, prompt, s_blk, t_blk, n_slots):
    if prompt:
        (x_ref, mk_ref, mv_ref, prev_a_ref, prev_b_ref, g1_ref, w_in_ref, caw_ref, cbw_ref, cbb_ref,
         lng_ref, lnb_ref, bias_ref, gng_ref, w_out_ref,
         x1_ref, new_a_ref, new_b_ref, k_out_ref, v_out_ref,
         hist_a, hist_b, *slots, kbuf, vbuf) = refs
    else:
        (x_ref, mk_ref, mv_ref, prev_a_ref, prev_b_ref, ck_ref, cv_ref, g1_ref, w_in_ref, caw_ref, cbw_ref,
         cbb_ref, lng_ref, lnb_ref, bias_ref, bias2_ref, gng_ref, w_out_ref,
         x1_ref, new_a_ref, new_b_ref, k_out_ref, v_out_ref,
         hist_a, hist_b, *slots) = refs
    h_scr, z_scr, y_scr = slots[:n_slots], slots[n_slots:2 * n_slots], slots[2 * n_slots:]
    sub = min(t_blk, SUB_TILE)
    n_sub = t_blk // sub
    rows = s_blk * sub
    t = pl.program_id(1)

    def group_norm(y, gi):
        return _rms(y, gng_ref[:, gi * G:(gi + 1) * G]).astype(BF16)

    @pl.when(t == 0)
    def _():
        hist_a[...] = prev_a_ref[...]
        hist_b[...] = prev_b_ref[...]

    ha = [hist_a[s] for s in range(s_blk)]
    hb = [hist_b[s] for s in range(s_blk)]

    def stage_in(i):
        slot, r0 = i % n_slots, i * sub
        x = x_ref[:, r0:r0 + sub, :].reshape(rows, D_MODEL)
        h_scr[slot][...] = _rms(x, g1_ref[...]).astype(BF16)
        yield
        for c in range(0, D_IN_PROJ, PROJ_COLS):
            z_scr[slot][:, c:c + PROJ_COLS] = _dot(h_scr[slot][...],w_in_ref[:, c:c + PROJ_COLS])
            yield

    def stage_conv(i):
        slot = i % n_slots
        caw = caw_ref[...]
        for s in range(s_blk):
            sl = slice(s * sub, (s + 1) * sub)
            p = z_scr[slot][sl, G:2 * G] * z_scr[slot][sl, 2 * G:3 * G]
            y_scr[slot][sl, 0:G] = group_norm(z_scr[slot][sl, 0:G] * _conv3(p, ha[s], caw), 0)
            ha[s] = p[sub - A_HEAD:]
        yield
        cbw = cbw_ref[...]
        first = B_HEAD - (CONV_B_WIDTH - 1)
        r_step = min(sub, CONV_B_ROWS)
        for s in range(s_blk):
            sl = slice(s * sub, (s + 1) * sub)
            glu = z_scr[slot][sl, 3 * G:4 * G] * jax.nn.sigmoid(z_scr[slot][sl, 4 * G:5 * G])
            xs = jnp.concatenate([hb[s], glu], axis=0)
            hb[s] = glu[sub - B_HEAD:]
            moved = [_roll_rows(xs, -b) for b in range(SUBLANES)]
            if s_blk == 1:
                yield
            for q0 in range(0, sub, r_step):
                acc = None
                for j in range(CONV_B_WIDTH):
                    off = first + j
                    lo = q0 + (off // SUBLANES) * SUBLANES
                    term = cbw[j:j + 1] * moved[off % SUBLANES][lo:lo + r_step]
                    acc = term if acc is None else acc + term
                c = acc + cbb_ref[...]
                cc = c - jnp.mean(c, axis=-1, keepdims=True)
                c = cc * lax.rsqrt(jnp.mean(cc * cc, axis=-1, keepdims=True) + LN_EPS) * lng_ref[...] + lnb_ref[...]
                y_scr[slot][s * sub + q0:s * sub + q0 + r_step, G:2 * G] = group_norm(_silu(c), 1)
                if s_blk == 1:
                    yield
            if s_blk > 1 and s % 2 == 1:
                yield

    def stage_attn(i):
        slot, r0 = i % n_slots, i * sub
        k_new = z_scr[slot][:, 6 * G:7 * G]
        v_new = z_scr[slot][:, 7 * G:8 * G]
        kv_off = t_blk - k_out_ref.shape[1]
        if r0 >= kv_off:
            k_out_ref[:, r0 - kv_off:r0 - kv_off + sub, :] = k_new.reshape(s_blk, sub, G)
            v_out_ref[:, r0 - kv_off:r0 - kv_off + sub, :] = v_new.reshape(s_blk, sub, G)
        if not prompt:
            for j in range(s_blk):
                sl = slice(j * sub, (j + 1) * sub)
                qs = _stack_heads(z_scr[slot][sl, 5 * G:6 * G] * Q_SCALE)
                kn = z_scr[slot][sl, 6 * G:7 * G].astype(BF16)
                vn = z_scr[slot][sl, 7 * G:8 * G].astype(BF16)
                s_old = _dot_nt(qs, ck_ref[j].astype(BF16)) + bias_ref[...]
                s_new = _dot_nt(qs, kn) + bias2_ref[...]
                (e_old, e_new), inv = _softmax_parts([s_old, s_new])
                o = (_dot(e_old.astype(BF16), cv_ref[j].astype(BF16)) + _dot(e_new.astype(BF16), vn)) * inv
                y_scr[slot][sl, 2 * G:3 * G] = group_norm(_unstack_heads(o), 2)
                qs = _stack_heads(z_scr[slot][sl, 8 * G:9 * G] * Q_SCALE)
                (e,), inv = _softmax_parts([_dot_nt(qs, mk_ref[j].astype(BF16))])
                o = _dot(e.astype(BF16), mv_ref[j].astype(BF16)) * inv
                y_scr[slot][sl, 3 * G:4 * G] = group_norm(_unstack_heads(o), 3)
                yield
            return

        if i == 0:
            @pl.when(t == 0)
            def _():
                kbuf[0:ATTN_WINDOW, :] = jnp.zeros((ATTN_WINDOW, G), BF16)
                vbuf[0:ATTN_WINDOW, :] = jnp.zeros((ATTN_WINDOW, G), BF16)

        row0 = pl.multiple_of(t * t_blk + r0, sub)
        kbuf[pl.ds(ATTN_WINDOW + row0, sub), :] = k_new.astype(BF16)
        vbuf[pl.ds(ATTN_WINDOW + row0, sub), :] = v_new.astype(BF16)
        yield
        for lo in range(0, sub, Q_PAIR):
            base = pl.multiple_of(row0 + lo, Q_PAIR)
            qs = _stack_heads(z_scr[slot][lo:lo + Q_PAIR, 5 * G:6 * G] * Q_SCALE)
            sc = _dot_nt(qs, kbuf[pl.ds(base, K_PAIR), :]) + bias_ref[...]
            if r0 + lo < ATTN_WINDOW:
                key_ok = (lax.broadcasted_iota(jnp.int32, (1, K_PAIR), 1) + base) >= ATTN_WINDOW
                sc = jnp.where(key_ok, sc, NEG_INF)
            (e,), inv = _softmax_parts([sc])
            yield
            o = _dot(e.astype(BF16), vbuf[pl.ds(base, K_PAIR), :]) * inv
            y_scr[slot][lo:lo + Q_PAIR, 2 * G:3 * G] = group_norm(_unstack_heads(o), 2)
            yield
            qs = _stack_heads(z_scr[slot][lo:lo + Q_PAIR, 8 * G:9 * G] * Q_SCALE)
            (e,), inv = _softmax_parts([_dot_nt(qs, mk_ref[0].astype(BF16))])
            yield
            o = _dot(e.astype(BF16), mv_ref[0].astype(BF16)) * inv
            y_scr[slot][lo:lo + Q_PAIR, 3 * G:4 * G] = group_norm(_unstack_heads(o), 3)
            yield

    def stage_out(i):
        slot, r0 = i % n_slots, i * sub
        for c in range(0, D_MODEL, PROJ_COLS):
            x = x_ref[:, r0:r0 + sub, c:c + PROJ_COLS].reshape(rows, PROJ_COLS)
            x1 = x + _dot(y_scr[slot][...],w_out_ref[:, c:c + PROJ_COLS])
            x1_ref[:, r0:r0 + sub, c:c + PROJ_COLS] = x1.reshape(s_blk, sub, PROJ_COLS)
            yield

    _interleave([stage_in(0)])
    for i in range(n_sub):
        stages, every = [stage_conv(i), stage_attn(i)], [1, 1]
        if i + 1 < n_sub:
            stages.append(stage_in(i + 1))
            every.append(1)
        if i > 0:
            stages.append(stage_out(i - 1))
            every.append(OUT_EVERY)
        _interleave(stages, every)
    _interleave([stage_out(n_sub - 1)])
    for s in range(s_blk):
        new_a_ref[s] = ha[s]
        new_b_ref[s] = hb[s]
        hist_a[s] = ha[s]
        hist_b[s] = hb[s]


def _const_spec(shape):
    zeros = (0,) * len(shape)
    return pl.BlockSpec(shape, lambda b, t: zeros)


def _layer_spec(layer, shape):
    zeros = (0,) * (len(shape) - 1)
    return pl.BlockSpec((None,) + tuple(shape[1:]), lambda b, t: (layer,) + zeros, pipeline_mode=pl.Buffered(1))


def _mixer(layer, x, mk, mv, prev_a, prev_b, cache_k, cache_v, weights, bias_tables, *, prompt):
    (g1, w_in, caw, cbw, cbb, lng, lnb, gng, w_out) = weights
    nb, seq, _ = x.shape
    if prompt:
        s_blk, t_blk = 1, PROMPT_TILE
    else:
        s_blk, t_blk = SAMPLE_STREAMS, seq
    grid = (nb // s_blk, seq // t_blk)
    nt = grid[1]
    rows = s_blk * t_blk

    stream_spec = lambda r, c: pl.BlockSpec((s_blk, r, c), lambda b, t: (b, 0, 0))
    layer_stream_spec = lambda r, c: pl.BlockSpec((None, s_blk, r, c), lambda b, t: (layer, b, 0, 0))
    tile_spec = lambda c: pl.BlockSpec((s_blk, t_blk, c), lambda b, t: (b, t, 0))
    in_specs = [tile_spec(D_MODEL), layer_stream_spec(N_MEM, G), layer_stream_spec(N_MEM, G),
                layer_stream_spec(A_HEAD, G), layer_stream_spec(B_HEAD, G)]
    args = [x, mk, mv, prev_a, prev_b]
    if not prompt:
        in_specs += [layer_stream_spec(ATTN_WINDOW, G), layer_stream_spec(ATTN_WINDOW, G)]
        args += [cache_k, cache_v]
    consts = [g1, w_in, caw, cbw, cbb, lng, lnb, *bias_tables, gng, w_out]
    in_specs += [_layer_spec(layer, c.shape) for c in consts]
    args += consts

    if prompt:
        kv_blk = min(t_blk, ATTN_WINDOW)
        first_kept = nt - ATTN_WINDOW // kv_blk
        kv_spec = pl.BlockSpec((1, kv_blk, G), lambda b, t: (b, jnp.maximum(t - first_kept, 0), 0))
        kv_rows = ATTN_WINDOW
    else:
        kv_spec = tile_spec(G)
        kv_rows = seq
    out_specs = [tile_spec(D_MODEL), stream_spec(A_HEAD, G), stream_spec(B_HEAD, G), kv_spec, kv_spec]
    out_shape = [jax.ShapeDtypeStruct((nb, seq, D_MODEL), F32),
                 jax.ShapeDtypeStruct((nb, A_HEAD, G), F32),
                 jax.ShapeDtypeStruct((nb, B_HEAD, G), F32),
                 jax.ShapeDtypeStruct((nb, kv_rows, G), F32),
                 jax.ShapeDtypeStruct((nb, kv_rows, G), F32)]
    sub_rows = s_blk * min(t_blk, SUB_TILE)
    n_slots = min(2, rows // sub_rows)
    scratch = ([pltpu.VMEM((s_blk, A_HEAD, G), F32), pltpu.VMEM((s_blk, B_HEAD, G), F32)]
               + [pltpu.VMEM((sub_rows, D_MODEL), BF16)] * n_slots
               + [pltpu.VMEM((sub_rows, D_IN_PROJ), F32)] * n_slots
               + [pltpu.VMEM((sub_rows, 4 * G), BF16)] * n_slots)
    if prompt:
        scratch += [pltpu.VMEM((ATTN_WINDOW + seq, G), BF16), pltpu.VMEM((ATTN_WINDOW + seq, G), BF16)]
    return pl.pallas_call(
        functools.partial(_mixer_body, prompt=prompt, s_blk=s_blk, t_blk=t_blk, n_slots=n_slots),
        grid=grid, in_specs=in_specs, out_specs=out_specs, out_shape=out_shape, scratch_shapes=scratch,
        compiler_params=pltpu.CompilerParams(dimension_semantics=("arbitrary", "arbitrary"),
                                             vmem_limit_bytes=VMEM_LIMIT),
        name="mixer_prompt" if prompt else "mixer_sample",
    )(*args)


def _ffn_body(x_ref, prev_ref, g2_ref, wu_ref, cw_ref, cb_ref, wd_ref, gf_ref,
              out_ref, new_f_ref, carry, *slots, s_blk, t_blk, final):
    sub = min(t_blk, SUB_TILE)
    n_sub = t_blk // sub
    n_slots = len(slots) // 2
    h_scr, act_scr = slots[:n_slots], slots[n_slots:]
    rows = s_blk * sub
    t = pl.program_id(1)

    @pl.when(t == 0)
    def _():
        carry[...] = prev_ref[...]

    cols = [c for j in range(N_FF_CHUNKS) for c in (j * FF_CHUNK, D_FF + j * FF_CHUNK)]
    hist = {c: [carry[s, :, c:c + FF_CHUNK] for s in range(s_blk)] for c in cols}

    def stage_gated(i):
        slot, r0 = i % n_slots, i * sub
        x = x_ref[:, r0:r0 + sub, :].reshape(rows, D_MODEL)
        h_scr[slot][...] = _rms(x, g2_ref[...]).astype(BF16)
        yield
        piece = min(sub, GATE_ROWS)
        for j in range(N_FF_CHUNKS):
            col_v, col_g = j * FF_CHUNK, D_FF + j * FF_CHUNK
            u_v = _dot(h_scr[slot][...],wu_ref[:, col_v:col_v + FF_CHUNK])
            u_g = _dot(h_scr[slot][...],wu_ref[:, col_g:col_g + FF_CHUNK])
            for s in range(s_blk):
                for q0 in range(s * sub, (s + 1) * sub, piece):
                    sides = []
                    for col, u in ((col_v, u_v), (col_g, u_g)):
                        up = u[q0:q0 + piece]
                        sides.append(_conv3(up, hist[col][s], cw_ref[:, col:col + FF_CHUNK])
                                     + cb_ref[:, col:col + FF_CHUNK])
                        hist[col][s] = up[piece - A_HEAD:]
                    val, gate = sides
                    act_scr[slot][q0:q0 + piece, col_v:col_v + FF_CHUNK] = (_silu(gate) * val).astype(BF16)
            yield

    def stage_down(i):
        slot, r0 = i % n_slots, i * sub
        for c in range(0, D_MODEL, PROJ_COLS):
            x = x_ref[:, r0:r0 + sub, c:c + PROJ_COLS].reshape(rows, PROJ_COLS)
            out = x + _dot(act_scr[slot][...],wd_ref[:, c:c + PROJ_COLS])
            out_ref[:, r0:r0 + sub, c:c + PROJ_COLS] = out.reshape(s_blk, sub, PROJ_COLS)
            yield
        if final:
            out = out_ref[:, r0:r0 + sub, :].reshape(rows, D_MODEL)
            out_ref[:, r0:r0 + sub, :] = _rms(out, gf_ref[...]).reshape(s_blk, sub, D_MODEL)
            yield

    _interleave([stage_gated(0)])
    for i in range(1, n_sub):
        _interleave([stage_gated(i), stage_down(i - 1)], every=(1, DOWN_EVERY))
    _interleave([stage_down(n_sub - 1)])

    for c in cols:
        for s in range(s_blk):
            new_f_ref[s, :, c:c + FF_CHUNK] = hist[c][s]
            carry[s, :, c:c + FF_CHUNK] = hist[c][s]


def _ffn(layer, x, prev_f, weights, final_g, *, prompt, final):
    (g2, wu, cw, cb, wd) = weights
    nb, seq, _ = x.shape
    if prompt:
        s_blk, t_blk = 1, FFN_TILE
    else:
        s_blk, t_blk = SAMPLE_STREAMS, seq
    grid = (nb // s_blk, seq // t_blk)
    sub_rows = s_blk * min(t_blk, SUB_TILE)
    n_slots = min(2, s_blk * t_blk // sub_rows)
    tile_spec = pl.BlockSpec((s_blk, t_blk, D_MODEL), lambda b, t: (b, t, 0))
    hist_spec = pl.BlockSpec((s_blk, A_HEAD, 2 * D_FF), lambda b, t: (b, 0, 0))
    prev_spec = pl.BlockSpec((None, s_blk, A_HEAD, 2 * D_FF), lambda b, t: (layer, b, 0, 0))
    consts = [g2, wu, cw, cb, wd]
    return pl.pallas_call(
        functools.partial(_ffn_body, s_blk=s_blk, t_blk=t_blk, final=final),
        grid=grid,
        in_specs=[tile_spec, prev_spec] + [_layer_spec(layer, c.shape) for c in consts] + [_const_spec(final_g.shape)],
        out_specs=[tile_spec, hist_spec],
        out_shape=[jax.ShapeDtypeStruct((nb, seq, D_MODEL), F32),
                   jax.ShapeDtypeStruct((nb, A_HEAD, 2 * D_FF), F32)],
        scratch_shapes=([pltpu.VMEM((s_blk, A_HEAD, 2 * D_FF), F32)]
                        + [pltpu.VMEM((sub_rows, D_MODEL), BF16)] * n_slots
                        + [pltpu.VMEM((sub_rows, D_FF), BF16)] * n_slots),
        compiler_params=pltpu.CompilerParams(dimension_semantics=("arbitrary", "arbitrary"),
                                             vmem_limit_bytes=VMEM_LIMIT),
        name="ffn_prompt" if prompt else "ffn_sample",
    )(x, prev_f, *consts, final_g)


def _rel_bias_toeplitz(rel_bias_l, n_q, n_k):
    lo = ATTN_WINDOW - (n_k - 1) + REL_CLIP
    hi = ATTN_WINDOW + (n_q - 1) + REL_CLIP
    pad_lo, pad_hi = max(0, -lo), max(0, hi - 2 * REL_CLIP)
    ext = jnp.pad(rel_bias_l, ((0, 0), (pad_lo, pad_hi)), mode="edge")[:, lo + pad_lo:hi + pad_lo + 1]
    period = n_q + n_k
    w = jnp.pad(ext[:, ::-1], ((0, 0), (0, 1)))
    flat = jnp.tile(w, (1, n_q))[:, :n_q * (period - 1)]
    return flat.reshape(-1, n_q, period - 1)[:, :, n_q - 1:n_q - 1 + n_k].astype(F32)


def _prompt_bias(rel_bias):
    r = np.arange(Q_PAIR)[:, None]
    kk = np.arange(K_PAIR)[None, :] - CHUNK * (r // CHUNK)
    in_band = (kk >= 0) & (kk < ATTN_WINDOW + CHUNK)
    depth = rel_bias.shape[0]
    tab = _rel_bias_toeplitz(rel_bias.reshape(depth * N_HEADS, -1), Q_PAIR, K_PAIR)
    return jnp.where(in_band[None], tab * LOG2E, NEG_INF).reshape(depth, N_HEADS * Q_PAIR, K_PAIR)


def _sample_bias(rel_bias, t):
    depth = rel_bias.shape[0]
    tab = _rel_bias_toeplitz(rel_bias.reshape(depth * N_HEADS, -1), t, ATTN_WINDOW + t)
    tab = tab.reshape(depth, N_HEADS * t, ATTN_WINDOW + t) * LOG2E
    return tab[:, :, :ATTN_WINDOW], tab[:, :, ATTN_WINDOW:]


def _pad_history(prev, head):
    return jnp.pad(prev, ((0, 0), (0, 0), (head - prev.shape[2], 0), (0, 0)))


def kernel(x_prompt, x_sample, cache_conv_a, cache_conv_b, cache_ffn_conv, cache_attn_k, cache_attn_v, cache_mem_k,
           cache_mem_v, mem_prompt, norm1_g, w_in, w_mem_kv, conv_a_w, conv_b_w, conv_b_bias, ln_b_g, ln_b_b, rel_bias,
           grp_norm_g, w_out, norm2_g, w_up, ffn_conv_w, ffn_conv_b, w_down, final_g):
    depth = w_in.shape[0]
    bp, tp, _ = x_prompt.shape
    bs, ts, _ = x_sample.shape
    assert tp % PROMPT_TILE == 0 and tp % FFN_TILE == 0 and SUB_TILE % Q_PAIR == 0 and tp >= ATTN_WINDOW
    assert bs % SAMPLE_STREAMS == 0 and ts % SUBLANES == 0 and ts >= B_HEAD
    assert cache_attn_k.shape[2] == ATTN_WINDOW

    row = lambda p: p.reshape(depth, 1, -1)
    mixer_w = (row(norm1_g), w_in.astype(BF16), conv_a_w, conv_b_w, row(conv_b_bias), row(ln_b_g), row(ln_b_b),
               row(grp_norm_g), w_out.astype(BF16))
    ffn_w = (row(norm2_g), w_up.astype(BF16), ffn_conv_w, row(ffn_conv_b), w_down.astype(BF16))
    final_row = final_g.reshape(1, D_MODEL)
    mem_k_all, mem_v_all = _memkv(mem_prompt, w_mem_kv.astype(BF16))

    prompt_bias = (_prompt_bias(rel_bias),)
    sample_bias = _sample_bias(rel_bias, ts)
    zero_a = jnp.zeros((depth, bp, A_HEAD, G), F32)
    zero_b = jnp.zeros((depth, bp, B_HEAD, G), F32)
    zero_f = jnp.zeros((depth, bp, A_HEAD, 2 * D_FF), F32)
    hist_a = _pad_history(cache_conv_a, A_HEAD)
    hist_b = _pad_history(cache_conv_b, B_HEAD)
    hist_f = _pad_history(cache_ffn_conv, A_HEAD)
    mem_k_s = cache_mem_k.reshape(depth, bs, N_MEM, G)
    mem_v_s = cache_mem_v.reshape(depth, bs, N_MEM, G)
    attn_k_s = cache_attn_k.reshape(depth, bs, ATTN_WINDOW, G)
    attn_v_s = cache_attn_v.reshape(depth, bs, ATTN_WINDOW, G)

    xp, xs = x_prompt, x_sample
    outs = {name: [] for name in ("pa", "pb", "pf", "pk", "pv", "sa", "sb", "sf", "sk", "sv")}
    for l in range(depth):
        final = l == depth - 1
        xp, na, nb, k_p, v_p = _mixer(l, xp, mem_k_all, mem_v_all, zero_a, zero_b, None, None, mixer_w,
                                      prompt_bias, prompt=True)
        xp, nf = _ffn(l, xp, zero_f, ffn_w, final_row, prompt=True, final=final)
        outs["pa"].append(na[:, A_HEAD - (CONV_A_WIDTH - 1):])
        outs["pb"].append(nb[:, B_HEAD - (CONV_B_WIDTH - 1):])
        outs["pf"].append(nf[:, A_HEAD - (FFN_CONV_WIDTH - 1):])
        outs["pk"].append(k_p.reshape(bp, ATTN_WINDOW, N_HEADS, HEAD_DIM))
        outs["pv"].append(v_p.reshape(bp, ATTN_WINDOW, N_HEADS, HEAD_DIM))

        xs, na, nb, k_s, v_s = _mixer(l, xs, mem_k_s, mem_v_s, hist_a, hist_b, attn_k_s, attn_v_s, mixer_w,
                                      sample_bias, prompt=False)
        xs, nf = _ffn(l, xs, hist_f, ffn_w, final_row, prompt=False, final=final)
        outs["sa"].append(na[:, A_HEAD - (CONV_A_WIDTH - 1):])
        outs["sb"].append(nb[:, B_HEAD - (CONV_B_WIDTH - 1):])
        outs["sf"].append(nf[:, A_HEAD - (FFN_CONV_WIDTH - 1):])
        outs["sk"].append(k_s.reshape(bs, ts, N_HEADS, HEAD_DIM))
        outs["sv"].append(v_s.reshape(bs, ts, N_HEADS, HEAD_DIM))

    st = {name: jnp.stack(v) for name, v in outs.items()}
    mem_shape = (depth, bp, N_MEM, N_HEADS, HEAD_DIM)
    return (xp, xs, st["pa"], st["pb"], st["pf"], st["pk"], st["pv"],
            mem_k_all.reshape(mem_shape), mem_v_all.reshape(mem_shape),
            st["sa"], st["sb"], st["sf"], st["sk"], st["sv"])
```
